```python
import jax, jax.numpy as jnp
from jax import lax
import numpy as np

D_MODEL = 2048
BATCH = 2
SEQ = 16384
DEPTH = 2

CTX_LEN = 256
GRID_W = 64
EPS = 1e-6

HEAD_DIM = 128
ATTN_W = 3 * D_MODEL // 4
N_Q_HEADS = ATTN_W // HEAD_DIM
N_KV_HEADS = 4
Q_GROUP = N_Q_HEADS // N_KV_HEADS
KV_W = N_KV_HEADS * HEAD_DIM
ROPE_THETA = 10000.0
Q_BLOCK = 128

POOL_WINDOWS = (2, 4, 8, 16)
POOL_W = D_MODEL // 4
POOL_GROUP = POOL_W // len(POOL_WINDOWS)

EVEN_IN = ATTN_W + 2 * KV_W + POOL_W
EVEN_MIX = ATTN_W + POOL_W

FFT_W = D_MODEL // 2
N_FFT_GROUPS = 4
FFT_GROUP = FFT_W // N_FFT_GROUPS

CONV_W = D_MODEL // 2
CONV_WIDTH = 3

ODD_IN = FFT_W + 3 * CONV_W
ODD_MIX = FFT_W + CONV_W

D_FF = -(-8 * D_MODEL // (3 * 256)) * 256
N_EVEN = (DEPTH + 1) // 2
N_ODD = DEPTH // 2

kernel_name = 'hybrid_attn_pool_fourier_conv_dit'


def rmsnorm(x, g):
    xf = x.astype(jnp.float32)
    y = xf * lax.rsqrt(jnp.mean(xf * xf, axis=-1, keepdims=True) + EPS)
    return (y * g.astype(jnp.float32)).astype(x.dtype)


def adaln(cvec, w, bias):
    return jnp.split(jax.nn.silu(cvec) @ w + bias, 6, axis=-1)


def modulate(h, shift, scale):
    return h * (1 + scale) + shift


def rope_tables(n):
    n_rows = n // GRID_W
    row = jnp.repeat(jnp.arange(n_rows, dtype=jnp.float32), GRID_W)
    col = jnp.tile(jnp.arange(GRID_W, dtype=jnp.float32), n_rows)
    quarter = HEAD_DIM // 4
    freqs = ROPE_THETA ** (-jnp.arange(quarter, dtype=jnp.float32) / quarter)
    ang = jnp.stack([row[:, None] * freqs, col[:, None] * freqs], axis=1)
    return jnp.cos(ang), jnp.sin(ang)


def apply_rope(x, cos, sin):
    b, n, h, d = x.shape
    xr = x.reshape(b, n, h, 2, 2, d // 4)
    cs = cos.astype(x.dtype)[None, :, None]
    sn = sin.astype(x.dtype)[None, :, None]
    x1, x2 = xr[..., 0, :], xr[..., 1, :]
    out = jnp.stack([x1 * cs - x2 * sn, x1 * sn + x2 * cs], axis=-2)
    return out.reshape(b, n, h, d)


def block_attention(q, k, v):
    b, nq = q.shape[:2]
    nb = nq // Q_BLOCK
    qb = q.reshape(b, nb, Q_BLOCK, N_KV_HEADS, Q_GROUP, HEAD_DIM).transpose(1, 0, 2, 3, 4, 5)
    scale = HEAD_DIM ** -0.5

    def one_block(qblk):
        s = jnp.einsum('bqhgd,bkhd->bhgqk', qblk, k).astype(jnp.float32) * scale
        p = jax.nn.softmax(s, axis=-1).astype(v.dtype)
        return jnp.einsum('bhgqk,bkhd->bqhgd', p, v)

    o = lax.map(one_block, qb)
    return o.transpose(1, 0, 2, 3, 4, 5).reshape(b, nq, ATTN_W)


def pool_mixer(u, w_pool, pool_scale):
    b, n, _ = u.shape
    uf = u.astype(jnp.float32)
    cs = jnp.concatenate([jnp.zeros((b, 1, POOL_W), jnp.float32), jnp.cumsum(uf, axis=1)], axis=1)
    t = jnp.arange(n)
    outs = []
    for g, w in enumerate(POOL_WINDOWS):
        sl = slice(g * POOL_GROUP, (g + 1) * POOL_GROUP)
        lo = jnp.clip(t - w // 2, 0, n)
        hi = jnp.clip(t - w // 2 + w, 0, n)
        mean = (cs[:, hi, sl] - cs[:, lo, sl]) / (hi - lo).astype(jnp.float32)[None, :, None]
        outs.append(mean - uf[..., sl])
    pooled = jnp.stack(outs, axis=2).astype(u.dtype)
    y = jnp.einsum('bngc,gcd->bngd', pooled, w_pool).reshape(b, n, POOL_W)
    return y * pool_scale


def attn_pool_mixer(h, hc, w_in, q_g, k_g, w_pool, pool_scale, cos, sin, ctx_full):
    b, n, _ = h.shape
    q, k, v, u = jnp.split(h @ w_in, [ATTN_W, ATTN_W + KV_W, ATTN_W + 2 * KV_W], axis=-1)
    q = apply_rope(rmsnorm(q.reshape(b, n, N_Q_HEADS, HEAD_DIM), q_g), cos, sin)
    k = apply_rope(rmsnorm(k.reshape(b, n, N_KV_HEADS, HEAD_DIM), k_g), cos, sin)
    v = v.reshape(b, n, N_KV_HEADS, HEAD_DIM)
    lc = hc.shape[1]
    if ctx_full:
        qc, kc, vc, uc = jnp.split(hc @ w_in, [ATTN_W, ATTN_W + KV_W, ATTN_W + 2 * KV_W], axis=-1)
    else:
        kc, vc = jnp.split(hc @ w_in[:, ATTN_W:ATTN_W + 2 * KV_W], 2, axis=-1)
    kc = rmsnorm(kc.reshape(b, lc, N_KV_HEADS, HEAD_DIM), k_g)
    vc = vc.reshape(b, lc, N_KV_HEADS, HEAD_DIM)
    k_all = jnp.concatenate([kc, k], axis=1)
    v_all = jnp.concatenate([vc, v], axis=1)
    out = jnp.concatenate([block_attention(q, k_all, v_all), pool_mixer(u, w_pool, pool_scale)], axis=-1)
    out_c = None
    if ctx_full:
        qc = rmsnorm(qc.reshape(b, lc, N_Q_HEADS, HEAD_DIM), q_g)
        out_c = jnp.concatenate([block_attention(qc, kc, vc), pool_mixer(uc, w_pool, pool_scale)], axis=-1)
    return out, out_c


def fourier_mixer(f):
    b, n, _ = f.shape
    fg = f.reshape(b, n, N_FFT_GROUPS, FFT_GROUP).astype(jnp.float32)
    y = jnp.fft.fft2(fg, axes=(1, 3), norm='ortho').real
    return y.reshape(b, n, FFT_W).astype(f.dtype)


def short_conv_mixer(bg, cg, xin, conv_w):
    n = xin.shape[1]
    pad = CONV_WIDTH // 2
    u = jnp.pad(cg * xin, ((0, 0), (pad, CONV_WIDTH - 1 - pad), (0, 0)))
    y = sum(u[:, j:j + n] * conv_w[j] for j in range(CONV_WIDTH))
    return bg * y


def fourier_conv_mixer(h, w_in, conv_w):
    f, bg, cg, xin = jnp.split(h @ w_in, [FFT_W, FFT_W + CONV_W, FFT_W + 2 * CONV_W], axis=-1)
    return jnp.concatenate([fourier_mixer(f), short_conv_mixer(bg, cg, xin, conv_w)], axis=-1)


def swiglu(h, wg, wu, wd):
    return (jax.nn.silu(h @ wg) * (h @ wu)) @ wd


def setup_inputs(seed: int = 0) -> dict:
    key = jax.random.key(seed)
    ks = jax.random.split(key, 24)
    nrm = jax.random.normal
    f32 = jnp.float32
    return {
        'x': nrm(ks[0], (BATCH, SEQ, D_MODEL), f32),
        'c': nrm(ks[1], (BATCH, D_MODEL), f32),
        'ctx': nrm(ks[2], (BATCH, CTX_LEN, D_MODEL), f32),
        'c_ctx': nrm(ks[3], (D_MODEL,), f32),
        'norm_g': 1.0 + 0.02 * nrm(ks[4], (DEPTH, 2, D_MODEL), f32),
        'w_mod': nrm(ks[5], (DEPTH, D_MODEL, 6 * D_MODEL), f32) * (0.5 * D_MODEL ** -0.5),
        'b_mod': 0.02 * nrm(ks[6], (DEPTH, 6 * D_MODEL), f32),
        'w_in_even': nrm(ks[7], (N_EVEN, D_MODEL, EVEN_IN), f32) * D_MODEL ** -0.5,
        'w_out_even': nrm(ks[8], (N_EVEN, EVEN_MIX, D_MODEL), f32) * EVEN_MIX ** -0.5,
        'q_norm_g': 1.0 + 0.02 * nrm(ks[9], (N_EVEN, HEAD_DIM), f32),
        'k_norm_g': 1.0 + 0.02 * nrm(ks[10], (N_EVEN, HEAD_DIM), f32),
        'w_pool': nrm(ks[11], (N_EVEN, len(POOL_WINDOWS), POOL_GROUP, POOL_GROUP), f32) * POOL_GROUP ** -0.5,
        'pool_scale': 1.0 + 0.02 * nrm(ks[12], (N_EVEN, POOL_W), f32),
        'w_in_odd': nrm(ks[13], (N_ODD, D_MODEL, ODD_IN), f32) * D_MODEL ** -0.5,
        'w_out_odd': nrm(ks[14], (N_ODD, ODD_MIX, D_MODEL), f32) * ODD_MIX ** -0.5,
        'conv_w': nrm(ks[15], (N_ODD, CONV_WIDTH, CONV_W), f32) * CONV_WIDTH ** -0.5,
        'w_ffn_gate': nrm(ks[16], (DEPTH, D_MODEL, D_FF), f32) * D_MODEL ** -0.5,
        'w_ffn_up': nrm(ks[17], (DEPTH, D_MODEL, D_FF), f32) * D_MODEL ** -0.5,
        'w_ffn_down': nrm(ks[18], (DEPTH, D_FF, D_MODEL), f32) * D_FF ** -0.5,
        'final_g': 1.0 + 0.02 * nrm(ks[19], (D_MODEL,), f32),
    }


def reference(x, c, ctx, c_ctx, norm_g, w_mod, b_mod, w_in_even, w_out_even, q_norm_g, k_norm_g,
              w_pool, pool_scale, w_in_odd, w_out_odd, conv_w, w_ffn_gate, w_ffn_up, w_ffn_down,
              final_g):
    n = x.shape[1]
    cos, sin = rope_tables(n)
    xc = ctx
    for i in range(DEPTH):
        is_even = i % 2 == 0
        update_ctx = any(j % 2 == 0 for j in range(i + 1, DEPTH))
        sh1, sc1, g1, sh2, sc2, g2 = [m[:, None, :] for m in adaln(c, w_mod[i], b_mod[i])]
        h = modulate(rmsnorm(x, norm_g[i, 0]), sh1, sc1)
        if is_even or update_ctx:
            csh1, csc1, cg1, csh2, csc2, cg2 = adaln(c_ctx, w_mod[i], b_mod[i])
            hc = modulate(rmsnorm(xc, norm_g[i, 0]), csh1, csc1)
        if is_even:
            e = i // 2
            mix, mix_c = attn_pool_mixer(h, hc, w_in_even[e], q_norm_g[e], k_norm_g[e], w_pool[e],
                                         pool_scale[e], cos, sin, update_ctx)
            w_out = w_out_even[e]
        else:
            o = i // 2
            mix = fourier_conv_mixer(h, w_in_odd[o], conv_w[o])
            mix_c = fourier_conv_mixer(hc, w_in_odd[o], conv_w[o]) if update_ctx else None
            w_out = w_out_odd[o]
        x = x + g1 * (mix @ w_out)
        x = x + g2 * swiglu(modulate(rmsnorm(x, norm_g[i, 1]), sh2, sc2),
                            w_ffn_gate[i], w_ffn_up[i], w_ffn_down[i])
        if update_ctx:
            xc = xc + cg1 * (mix_c @ w_out)
            xc = xc + cg2 * swiglu(modulate(rmsnorm(xc, norm_g[i, 1]), csh2, csc2),
                                   w_ffn_gate[i], w_ffn_up[i], w_ffn_down[i])
    return rmsnorm(x, final_g)
```

```python
import functools
import math

import jax
import jax.numpy as jnp
from jax import lax
from jax.experimental import pallas as pl
from jax.experimental.pallas import tpu as pltpu

F32 = jnp.float32
BF16 = jnp.bfloat16

D_MODEL = 2048
CTX_LEN = 256
GRID_W = 64
EPS = 1e-6
HEAD_DIM = 128
N_Q_HEADS = 12
N_KV_HEADS = 4
Q_GROUP = N_Q_HEADS // N_KV_HEADS
ATTN_W = N_Q_HEADS * HEAD_DIM
KV_W = N_KV_HEADS * HEAD_DIM
ROPE_THETA = 10000.0
POOL_WINDOWS = (2, 4, 8, 16)
POOL_W = 512
POOL_GROUP = 128
FFT_W = 1024
N_FFT_GROUPS = 4
FFT_GROUP = 256
CONV_W = 1024
HALO = 8

V7X_VMEM_BYTES = 64 * 1024 * 1024
VMEM_LIMIT = V7X_VMEM_BYTES - 8 * 1024 * 1024

QK_SCALE_LOG2 = HEAD_DIM ** -0.5 * math.log2(math.e)
SAFE_LOGIT_BOUND = 80.0

NT_DIMS = (((1,), (1,)), ((), ()))


def _params(semantics):
    return pltpu.CompilerParams(dimension_semantics=semantics, vmem_limit_bytes=VMEM_LIMIT)


def _dot(a, b):
    return jnp.dot(a, b, preferred_element_type=F32)


def _rms(x, g):
    ms = jnp.mean(x * x, axis=-1, keepdims=True)
    return x * lax.rsqrt(ms + EPS) * g


def _silu(x):
    return x / (1.0 + jnp.exp(-x))


def _resident(shape):
    zeros = (0,) * len(shape)
    return pl.BlockSpec(shape, lambda *_: zeros, pipeline_mode=pl.Buffered(1))


def _mod_kernel(c_ref, w_ref, b_ref, o_ref):
    s = _silu(c_ref[...]).astype(BF16)
    o_ref[...] = _dot(s, w_ref[...].astype(BF16)) + b_ref[...]


def _modulation(cvec, w_mod, b_mod):
    depth, d, n_out = w_mod.shape
    tn = 1024
    return pl.pallas_call(
        _mod_kernel,
        grid=(depth, n_out // tn),
        in_specs=[
            pl.BlockSpec((8, d), lambda l, j: (0, 0)),
            pl.BlockSpec((None, d, tn), lambda l, j: (l, 0, j)),
            pl.BlockSpec((None, 1, tn), lambda l, j: (l, 0, j)),
        ],
        out_specs=pl.BlockSpec((None, 8, tn), lambda l, j: (l, 0, j)),
        out_shape=jax.ShapeDtypeStruct((depth, 8, n_out), F32),
        compiler_params=_params(("arbitrary", "arbitrary")),
        name="adaln_modulation",
    )(cvec, w_mod, b_mod.reshape(depth, 1, n_out))


def _in_even_kernel(x_ref, sh_ref, sc_ref, ng_ref, w_ref, wvt_ref, qg_ref, kg_ref, cos_ref, sin_ref,
                    q_ref, k_ref, vt_ref, u_ref):
    h = _rms(x_ref[...], ng_ref[...]) * (1.0 + sc_ref[...]) + sh_ref[...]
    hb = h.astype(BF16)
    cosf = cos_ref[...]
    sinf = sin_ref[...]
    lane = lax.broadcasted_iota(jnp.int32, cosf.shape, 1)
    first_of_pair = (lane & 32) == 0

    def norm_rope(z, g):
        zn = _rms(z, g)
        partner = jnp.where(first_of_pair, pltpu.roll(zn, 96, 1), pltpu.roll(zn, 32, 1))
        return zn * cosf + partner * sinf

    qg = qg_ref[...]
    kg = kg_ref[...]
    heads_per_chunk = 4
    cw = heads_per_chunk * HEAD_DIM
    for c in range(ATTN_W // cw):
        z = _dot(hb, w_ref[:, c * cw:(c + 1) * cw])
        for j in range(heads_per_chunk):
            col = c * cw + j * HEAD_DIM
            zh = norm_rope(z[:, j * HEAD_DIM:(j + 1) * HEAD_DIM], qg) * QK_SCALE_LOG2
            q_ref[:, col:col + HEAD_DIM] = zh.astype(BF16)
    z = _dot(hb, w_ref[:, ATTN_W:ATTN_W + KV_W])
    for j in range(N_KV_HEADS):
        zh = norm_rope(z[:, j * HEAD_DIM:(j + 1) * HEAD_DIM], kg)
        k_ref[:, j * HEAD_DIM:(j + 1) * HEAD_DIM] = zh.astype(BF16)
    u_ref[...] = _dot(hb, w_ref[:, ATTN_W + KV_W:])
    vt_ref[...] = lax.dot_general(wvt_ref[...], hb, NT_DIMS, preferred_element_type=F32).astype(BF16)


def _in_proj_even(x, shift, scale, ng, w_qku, w_vt, qg, kg, cos_t, sin_t, tm):
    b, n, d = x.shape
    row = lambda bi, i: (bi, i, 0)
    vec = lambda bi, i: (bi, 0, 0)
    return pl.pallas_call(
        _in_even_kernel,
        grid=(b, n // tm),
        in_specs=[
            pl.BlockSpec((None, tm, d), row),
            pl.BlockSpec((None, 1, d), vec),
            pl.BlockSpec((None, 1, d), vec),
            _resident((1, d)),
            _resident(w_qku.shape),
            _resident(w_vt.shape),
            _resident((1, HEAD_DIM)),
            _resident((1, HEAD_DIM)),
            pl.BlockSpec((tm, HEAD_DIM), lambda bi, i: (i, 0)),
            pl.BlockSpec((tm, HEAD_DIM), lambda bi, i: (i, 0)),
        ],
        out_specs=[
            pl.BlockSpec((None, tm, ATTN_W), row),
            pl.BlockSpec((None, tm, KV_W), row),
            pl.BlockSpec((None, KV_W, tm), lambda bi, i: (bi, 0, i)),
            pl.BlockSpec((None, tm, POOL_W), row),
        ],
        out_shape=[
            jax.ShapeDtypeStruct((b, n, ATTN_W), BF16),
            jax.ShapeDtypeStruct((b, n, KV_W), BF16),
            jax.ShapeDtypeStruct((b, KV_W, n), BF16),
            jax.ShapeDtypeStruct((b, n, POOL_W), F32),
        ],
        compiler_params=_params(("arbitrary", "arbitrary")),
        name="in_proj_even",
    )(x, shift, scale, ng, w_qku, w_vt, qg, kg, cos_t, sin_t)


def _flash_kernel(q_ref, k_ref, vt_ref, kc_ref, vct_ref, o_ref, acc_ref, l_ref, m_ref, *, tk, stabilize):
    tq = q_ref.shape[0]
    n_keys = k_ref.shape[0]
    q3 = q_ref[...]
    qcat = jnp.concatenate([q3[:, g * HEAD_DIM:(g + 1) * HEAD_DIM] for g in range(Q_GROUP)], axis=0)

    def step(kt, vtt, first):
        s = lax.dot_general(kt, qcat, NT_DIMS, preferred_element_type=F32)
        if stabilize:
            mx = jnp.max(s, axis=0, keepdims=True)
            m_new = mx if first else jnp.maximum(m_ref[...], mx)
            p = jnp.exp2(s - m_new)
        else:
            p = jnp.exp2(s)
        ps = jnp.sum(p, axis=0, keepdims=True)
        pv = _dot(vtt, p.astype(BF16))
        if first:
            acc_ref[...] = pv
            l_ref[...] = ps
        elif stabilize:
            alpha = jnp.exp2(m_ref[...] - m_new)
            acc_ref[...] = acc_ref[...] * alpha + pv
            l_ref[...] = l_ref[...] * alpha + ps
        else:
            acc_ref[...] += pv
            l_ref[...] += ps
        if stabilize:
            m_ref[...] = m_new

    step(kc_ref[...], vct_ref[...], True)

    def body(j, carry):
        off = pl.multiple_of(j * tk, tk)
        step(k_ref[pl.ds(off, tk), :], vt_ref[:, pl.ds(off, tk)], False)
        return carry

    lax.fori_loop(0, n_keys // tk, body, 0)
    o = acc_ref[...] / l_ref[...]
    for g in range(Q_GROUP):
        o_ref[:, g * HEAD_DIM:(g + 1) * HEAD_DIM] = o[:, g * tq:(g + 1) * tq].T.astype(BF16)


def _attention(q, k, vt, kc, vct, *, tq, tk, stabilize):
    b, n, _ = q.shape
    lc = kc.shape[1]
    gw = Q_GROUP * HEAD_DIM
    r = Q_GROUP * tq
    return pl.pallas_call(
        functools.partial(_flash_kernel, tk=tk, stabilize=stabilize),
        grid=(b, N_KV_HEADS, n // tq),
        in_specs=[
            pl.BlockSpec((None, tq, gw), lambda bi, h, i: (bi, i, h)),
            pl.BlockSpec((None, n, HEAD_DIM), lambda bi, h, i: (bi, 0, h)),
            pl.BlockSpec((None, HEAD_DIM, n), lambda bi, h, i: (bi, h, 0)),
            pl.BlockSpec((None, lc, HEAD_DIM), lambda bi, h, i: (bi, 0, h)),
            pl.BlockSpec((None, HEAD_DIM, lc), lambda bi, h, i: (bi, h, 0)),
        ],
        out_specs=pl.BlockSpec((None, tq, gw), lambda bi, h, i: (bi, i, h)),
        out_shape=jax.ShapeDtypeStruct((b, n, ATTN_W), BF16),
        scratch_shapes=[
            pltpu.VMEM((HEAD_DIM, r), F32),
            pltpu.VMEM((1, r), F32),
            pltpu.VMEM((1, r), F32),
        ],
        compiler_params=_params(("arbitrary", "arbitrary", "arbitrary")),
        name="attention_stable" if stabilize else "attention",
    )(q, k, vt, kc, vct)


def _fill_halo(ext_ref, prev_ref, cur_ref, next_ref, i, n_tiles):
    tm = cur_ref.shape[0]
    ext_ref[0:HALO, :] = jnp.where(i > 0, prev_ref[...], 0.0)
    ext_ref[HALO:HALO + tm, :] = cur_ref[...]
    ext_ref[HALO + tm:, :] = jnp.where(i < n_tiles - 1, next_ref[...], 0.0)


def _resid_norm(x, delta, gate, ng, shift, scale):
    x_new = x + gate * delta
    return x_new, _rms(x_new, ng) * (1.0 + scale) + shift


def _out_even_kernel(o_ref, uprev_ref, u_ref, unext_ref, wpool_ref, pscale_ref, wout_ref,
                     x_ref, gate_ref, ng_ref, sh_ref, sc_ref,
                     xo_ref, ho_ref, ext_ref, *, seq_len):
    i = pl.program_id(1)
    tm = u_ref.shape[0]
    _fill_halo(ext_ref, uprev_ref, u_ref, unext_ref, i, pl.num_programs(1))
    t = i * tm + lax.broadcasted_iota(jnp.int32, (tm, POOL_GROUP), 0)
    delta = _dot(o_ref[...], wout_ref[0:ATTN_W, :])
    for g, w in enumerate(POOL_WINDOWS):
        cols = slice(g * POOL_GROUP, (g + 1) * POOL_GROUP)
        win = ext_ref[HALO - w // 2:HALO - w // 2 + tm, cols]
        for dlt in range(-w // 2 + 1, w // 2):
            win = win + ext_ref[HALO + dlt:HALO + dlt + tm, cols]
        lo = jnp.clip(t - w // 2, 0, seq_len)
        hi = jnp.clip(t - w // 2 + w, 0, seq_len)
        pooled = win / (hi - lo).astype(F32) - u_ref[:, cols]
        y = _dot(pooled.astype(BF16), wpool_ref[g]) * pscale_ref[:, cols]
        delta = delta + _dot(y.astype(BF16), wout_ref[ATTN_W + g * POOL_GROUP:ATTN_W + (g + 1) * POOL_GROUP, :])
    x_new, h_new = _resid_norm(x_ref[...], delta, gate_ref[...], ng_ref[...], sh_ref[...], sc_ref[...])
    xo_ref[...] = x_new
    ho_ref[...] = h_new.astype(BF16)


def _out_odd_kernel(y_ref, cprev_ref, c_ref, cnext_ref, bg_ref, cw_ref, wout_ref,
                    x_ref, gate_ref, ng_ref, sh_ref, sc_ref,
                    xo_ref, ho_ref, ext_ref):
    i = pl.program_id(1)
    tm = c_ref.shape[0]
    _fill_halo(ext_ref, cprev_ref, c_ref, cnext_ref, i, pl.num_programs(1))
    conv = (ext_ref[HALO - 1:HALO - 1 + tm, :] * cw_ref[0:1, :]
            + ext_ref[HALO:HALO + tm, :] * cw_ref[1:2, :]
            + ext_ref[HALO + 1:HALO + 1 + tm, :] * cw_ref[2:3, :])
    gated = (bg_ref[...] * conv).astype(BF16)
    delta = _dot(y_ref[...].astype(BF16), wout_ref[0:FFT_W, :]) + _dot(gated, wout_ref[FFT_W:, :])
    x_new, h_new = _resid_norm(x_ref[...], delta, gate_ref[...], ng_ref[...], sh_ref[...], sc_ref[...])
    xo_ref[...] = x_new
    ho_ref[...] = h_new.astype(BF16)


def _halo_specs(width, tm, n):
    tiles8 = tm // HALO
    last8 = n // HALO - 1
    return [
        pl.BlockSpec((None, HALO, width), lambda bi, i: (bi, jnp.maximum(i * tiles8 - 1, 0), 0)),
        pl.BlockSpec((None, tm, width), lambda bi, i: (bi, i, 0)),
        pl.BlockSpec((None, HALO, width), lambda bi, i: (bi, jnp.minimum((i + 1) * tiles8, last8), 0)),
    ]


def _resid_specs(tm, d):
    row = pl.BlockSpec((None, tm, d), lambda bi, i: (bi, i, 0))
    vec = pl.BlockSpec((None, 1, d), lambda bi, i: (bi, 0, 0))
    return [row, vec, _resident((1, d)), vec, vec]


def _resid_outs(b, n, d, tm):
    row = lambda bi, i: (bi, i, 0)
    specs = [pl.BlockSpec((None, tm, d), row), pl.BlockSpec((None, tm, d), row)]
    shapes = [jax.ShapeDtypeStruct((b, n, d), F32), jax.ShapeDtypeStruct((b, n, d), BF16)]
    return specs, shapes


def _out_proj_even(o, u, w_pool, pool_scale, w_out, x, gate, ng, shift, scale, tm):
    b, n, d = x.shape
    out_specs, out_shape = _resid_outs(b, n, d, tm)
    return pl.pallas_call(
        functools.partial(_out_even_kernel, seq_len=n),
        grid=(b, n // tm),
        in_specs=[pl.BlockSpec((None, tm, ATTN_W), lambda bi, i: (bi, i, 0))]
        + _halo_specs(POOL_W, tm, n)
        + [_resident(w_pool.shape), _resident((1, POOL_W)), _resident(w_out.shape)]
        + _resid_specs(tm, d),
        out_specs=out_specs,
        out_shape=out_shape,
        scratch_shapes=[pltpu.VMEM((tm + 2 * HALO, POOL_W), F32)],
        compiler_params=_params(("arbitrary", "arbitrary")),
        name="out_proj_even",
    )(o, u, u, u, w_pool, pool_scale, w_out, x, gate, ng, shift, scale)


def _out_proj_odd(y, cu, bg, conv_w, w_out, x, gate, ng, shift, scale, tm):
    b, n, d = x.shape
    out_specs, out_shape = _resid_outs(b, n, d, tm)
    row = lambda bi, i: (bi, i, 0)
    return pl.pallas_call(
        _out_odd_kernel,
        grid=(b, n // tm),
        in_specs=[pl.BlockSpec((None, tm, FFT_W), row)]
        + _halo_specs(CONV_W, tm, n)
        + [pl.BlockSpec((None, tm, CONV_W), row), _resident(conv_w.shape), _resident(w_out.shape)]
        + _resid_specs(tm, d),
        out_specs=out_specs,
        out_shape=out_shape,
        scratch_shapes=[pltpu.VMEM((tm + 2 * HALO, CONV_W), F32)],
        compiler_params=_params(("arbitrary", "arbitrary")),
        name="out_proj_odd",
    )(y, cu, cu, cu, bg, conv_w, w_out, x, gate, ng, shift, scale)


def _ffn_kernel(h_ref, wg_ref, wu_ref, wd_ref, x_ref, gate_ref, ng_ref, sh_ref, sc_ref,
                *refs, emit_x):
    if emit_x:
        xo_ref, ho_ref, acc_ref = refs
    else:
        ho_ref, acc_ref = refs
    f = pl.program_id(2)
    h = h_ref[...]
    a = _silu(_dot(h, wg_ref[...])) * _dot(h, wu_ref[...])
    part = _dot(a.astype(BF16), wd_ref[...])

    @pl.when(f == 0)
    def _():
        acc_ref[...] = part

    @pl.when(f > 0)
    def _():
        acc_ref[...] += part

    @pl.when(f == pl.num_programs(2) - 1)
    def _():
        x_new, h_new = _resid_norm(x_ref[...], acc_ref[...], gate_ref[...], ng_ref[...],
                                   sh_ref[...], sc_ref[...])
        if emit_x:
            xo_ref[...] = x_new
        ho_ref[...] = h_new.astype(ho_ref.dtype)


def _ffn(h, wg, wu, wd, x, gate, ng, shift, scale, *, tm, tf, emit_x):
    b, n, d = x.shape
    d_ff = wg.shape[1]
    row = lambda bi, i, f: (bi, i, 0)
    vec = lambda bi, i, f: (bi, 0, 0)
    out_specs = [pl.BlockSpec((None, tm, d), row)]
    out_shape = [jax.ShapeDtypeStruct((b, n, d), BF16 if emit_x else F32)]
    if emit_x:
        out_specs = [pl.BlockSpec((None, tm, d), row)] + out_specs
        out_shape = [jax.ShapeDtypeStruct((b, n, d), F32)] + out_shape
    return pl.pallas_call(
        functools.partial(_ffn_kernel, emit_x=emit_x),
        grid=(b, n // tm, d_ff // tf),
        in_specs=[
            pl.BlockSpec((None, tm, d), row),
            pl.BlockSpec((d, tf), lambda bi, i, f: (0, f)),
            pl.BlockSpec((d, tf), lambda bi, i, f: (0, f)),
            pl.BlockSpec((tf, d), lambda bi, i, f: (f, 0)),
            pl.BlockSpec((None, tm, d), row),
            pl.BlockSpec((None, 1, d), vec),
            pl.BlockSpec((1, d), lambda bi, i, f: (0, 0)),
            pl.BlockSpec((None, 1, d), vec),
            pl.BlockSpec((None, 1, d), vec),
        ],
        out_specs=out_specs,
        out_shape=out_shape,
        scratch_shapes=[pltpu.VMEM((tm, d), F32)],
        compiler_params=_params(("arbitrary", "arbitrary", "arbitrary")),
        name="ffn_swiglu" if emit_x else "ffn_swiglu_final",
    )(h, wg, wu, wd, x, gate, ng, shift, scale)


def _in_odd_kernel(h_ref, w_ref, cs_ref, zr_ref, zi_ref, cu_ref, bg_ref):
    h = h_ref[...]
    for g in range(N_FFT_GROUPS):
        cols = slice(g * FFT_GROUP, (g + 1) * FFT_GROUP)
        fg = _dot(h, w_ref[:, cols]).astype(BF16)
        z = _dot(fg, cs_ref[...])
        zr_ref[:, cols] = z[:, :FFT_GROUP].astype(BF16)
        zi_ref[:, cols] = z[:, FFT_GROUP:].astype(BF16)
    cw = 512
    for c in range(CONV_W // cw):
        lo = c * cw
        bg_ref[:, lo:lo + cw] = _dot(h, w_ref[:, FFT_W + lo:FFT_W + lo + cw])
        cg = _dot(h, w_ref[:, FFT_W + CONV_W + lo:FFT_W + CONV_W + lo + cw])
        xin = _dot(h, w_ref[:, FFT_W + 2 * CONV_W + lo:FFT_W + 2 * CONV_W + lo + cw])
        cu_ref[:, lo:lo + cw] = cg * xin


def _in_proj_odd(h, w_in, cs, tm):
    b, n, d = h.shape
    row = lambda bi, i: (bi, i, 0)
    return pl.pallas_call(
        _in_odd_kernel,
        grid=(b, n // tm),
        in_specs=[pl.BlockSpec((None, tm, d), row), _resident(w_in.shape), _resident(cs.shape)],
        out_specs=[pl.BlockSpec((None, tm, FFT_W), row)] * 2 + [pl.BlockSpec((None, tm, CONV_W), row)] * 2,
        out_shape=[jax.ShapeDtypeStruct((b, n, FFT_W), BF16)] * 2
        + [jax.ShapeDtypeStruct((b, n, CONV_W), F32)] * 2,
        compiler_params=_params(("arbitrary", "arbitrary")),
        name="in_proj_odd",
    )(h, w_in, cs)


def _dft1_kernel(zr_ref, zi_ref, w_ref, tr_ref, ti_ref):
    zz = jnp.concatenate([zr_ref[...], zi_ref[...]], axis=0)
    t = _dot(w_ref[...], zz)
    half = zr_ref.shape[0]
    tr_ref[...] = t[:half].astype(BF16)
    ti_ref[...] = t[half:].astype(BF16)


def _dft_stage1(zr, zi, w1, tn):
    b, n1, cols = zr.shape
    blk = pl.BlockSpec((None, n1, tn), lambda bi, j: (bi, 0, j))
    return pl.pallas_call(
        _dft1_kernel,
        grid=(b, cols // tn),
        in_specs=[blk, blk, _resident(w1.shape)],
        out_specs=[blk, blk],
        out_shape=[jax.ShapeDtypeStruct(zr.shape, BF16)] * 2,
        compiler_params=_params(("arbitrary", "arbitrary")),
        name="dft_stage1",
    )(zr, zi, w1)


def _dft2_kernel(tr_ref, ti_ref, gr_ref, gi_ref, y_ref):
    y = _dot(gr_ref[...], tr_ref[...]) + _dot(gi_ref[...], ti_ref[...])
    y_ref[...] = y.reshape(y_ref.shape)


def _dft_stage2(tr, ti, gr, gi):
    b, n, ch = tr.shape
    n_blk, rows, _ = gr.shape
    k2 = rows // HALO
    t_spec = pl.BlockSpec((None, rows, ch), lambda bi, j: (bi, j, 0))
    g_spec = pl.BlockSpec((None, rows, rows), lambda bi, j: (j, 0, 0))
    return pl.pallas_call(
        _dft2_kernel,
        grid=(b, n_blk),
        in_specs=[t_spec, t_spec, g_spec, g_spec],
        out_specs=pl.BlockSpec((None, k2, HALO, ch), lambda bi, j: (bi, 0, j, 0)),
        out_shape=jax.ShapeDtypeStruct((b, k2, n // k2, ch), F32),
        compiler_params=_params(("arbitrary", "arbitrary")),
        name="dft_stage2",
    )(tr, ti, gr, gi)


def _rope_tables(n):
    n_rows = n // GRID_W
    row = jnp.repeat(jnp.arange(n_rows, dtype=F32), GRID_W)
    col = jnp.tile(jnp.arange(GRID_W, dtype=F32), n_rows)
    quarter = HEAD_DIM // 4
    freqs = ROPE_THETA ** (-jnp.arange(quarter, dtype=F32) / quarter)
    ar = row[:, None] * freqs
    ac = col[:, None] * freqs
    cos_t = jnp.concatenate([jnp.cos(ar), jnp.cos(ar), jnp.cos(ac), jnp.cos(ac)], axis=-1)
    sin_t = jnp.concatenate([-jnp.sin(ar), jnp.sin(ar), -jnp.sin(ac), jnp.sin(ac)], axis=-1)
    return cos_t, sin_t


def _unit_angle(num, den):
    return (2.0 * math.pi / den) * (num % den).astype(F32)


def _dft_tables(n):
    n1 = 128
    n2 = n // n1
    ch = jnp.arange(FFT_GROUP, dtype=jnp.int32)
    a = _unit_angle(ch[:, None] * ch[None, :], FFT_GROUP)
    cs = jnp.concatenate([jnp.cos(a), -jnp.sin(a)], axis=1).astype(BF16)
    k1 = jnp.arange(n1, dtype=jnp.int32)
    a1 = _unit_angle(k1[:, None] * k1[None, :], n1)
    fr, fi = jnp.cos(a1), -jnp.sin(a1)
    w1 = jnp.concatenate([jnp.concatenate([fr, -fi], axis=1),
                          jnp.concatenate([fi, fr], axis=1)], axis=0).astype(BF16)
    k = jnp.arange(n, dtype=jnp.int32)
    a2 = _unit_angle(k[:, None] * jnp.arange(n2, dtype=jnp.int32)[None, :], n)
    norm = 1.0 / math.sqrt(n * FFT_GROUP)
    eye = jnp.eye(HALO, dtype=F32)

    def expand(e):
        e = e.reshape(n // n1, n1 // HALO, HALO, n2).transpose(1, 0, 2, 3)
        e = e[:, :, :, None, :] * eye[None, None, :, :, None]
        return e.reshape(n1 // HALO, (n // n1) * HALO, HALO * n2).astype(BF16)

    return cs, w1, expand(jnp.cos(a2) * norm), expand(jnp.sin(a2) * norm)


def kernel(x, c, ctx, c_ctx, norm_g, w_mod, b_mod, w_in_even, w_out_even, q_norm_g, k_norm_g, w_pool,
           pool_scale, w_in_odd, w_out_odd, conv_w, w_ffn_gate, w_ffn_up, w_ffn_down, final_g):
    b, n, d = x.shape
    tm = 512

    cvec = jnp.zeros((8, d), F32).at[:b].set(c).at[b].set(c_ctx)
    mod = _modulation(cvec, w_mod, b_mod).reshape(w_mod.shape[0], 8, 6, 1, d)

    def mvec(layer, comp, rows=slice(0, b)):
        return mod[layer, rows, comp]

    zeros_vec = jnp.zeros((b, 1, d), F32)

    w0 = w_in_even[0]
    w_qku = jnp.concatenate([w0[:, :ATTN_W + KV_W], w0[:, ATTN_W + 2 * KV_W:]], axis=1).astype(BF16)
    w_vt = w0[:, ATTN_W + KV_W:ATTN_W + 2 * KV_W].T.astype(BF16)
    qg = q_norm_g[0].reshape(1, HEAD_DIM)
    kg = k_norm_g[0].reshape(1, HEAD_DIM)
    ng00 = norm_g[0, 0].reshape(1, d)
    cos_t, sin_t = _rope_tables(n)
    q, k, vt, u = _in_proj_even(x, mvec(0, 0), mvec(0, 1), ng00, w_qku, w_vt, qg, kg, cos_t, sin_t, tm)

    lc = ctx.shape[1]
    ctx_rows = slice(b, b + 1)
    ctx_shift = jnp.broadcast_to(mvec(0, 0, ctx_rows), (b, 1, d))
    ctx_scale = jnp.broadcast_to(mvec(0, 1, ctx_rows), (b, 1, d))
    ident_cos = jnp.ones((lc, HEAD_DIM), F32)
    ident_sin = jnp.zeros((lc, HEAD_DIM), F32)
    _, kc, vct, _ = _in_proj_even(ctx, ctx_shift, ctx_scale, ng00, w_qku, w_vt, qg, kg,
                                  ident_cos, ident_sin, lc)

    logit_bound = (math.sqrt(HEAD_DIM) * math.log2(math.e) * 1.02
                   * jnp.max(jnp.abs(q_norm_g[0])) * jnp.max(jnp.abs(k_norm_g[0])))
    attn = functools.partial(_attention, q, k, vt, kc, vct, tq=256, tk=1024)
    o = lax.cond(logit_bound <= SAFE_LOGIT_BOUND,
                 lambda: attn(stabilize=False), lambda: attn(stabilize=True))

    x1, h1 = _out_proj_even(o, u, w_pool[0].astype(BF16), pool_scale[0].reshape(1, POOL_W),
                            w_out_even[0].astype(BF16), x, mvec(0, 2), norm_g[0, 1].reshape(1, d),
                            mvec(0, 3), mvec(0, 4), tm)
    x2, h2 = _ffn(h1, w_ffn_gate[0].astype(BF16), w_ffn_up[0].astype(BF16), w_ffn_down[0].astype(BF16),
                  x1, mvec(0, 5), norm_g[1, 0].reshape(1, d), mvec(1, 0), mvec(1, 1),
                  tm=tm, tf=512, emit_x=True)

    cs, w1, gr, gi = _dft_tables(n)
    zr, zi, cu, bg = _in_proj_odd(h2, w_in_odd[0].astype(BF16), cs, tm)
    n1 = 128
    tr, ti = _dft_stage1(zr.reshape(b, n1, -1), zi.reshape(b, n1, -1), w1, 4096)
    y = _dft_stage2(tr.reshape(b, n, FFT_W), ti.reshape(b, n, FFT_W), gr, gi).reshape(b, n, FFT_W)
    x3, h3 = _out_proj_odd(y, cu, bg, conv_w[0], w_out_odd[0].astype(BF16), x2, mvec(1, 2),
                           norm_g[1, 1].reshape(1, d), mvec(1, 3), mvec(1, 4), tm)
    (out,) = _ffn(h3, w_ffn_gate[1].astype(BF16), w_ffn_up[1].astype(BF16), w_ffn_down[1].astype(BF16),
                  x3, mvec(1, 5), final_g.reshape(1, d), zeros_vec, zeros_vec,
                  tm=tm, tf=512, emit_x=False)
    return out
```

```python
import functools
import math

import jax
import jax.numpy as jnp
from jax import lax
from jax.experimental import pallas as pl
from jax.experimental.pallas import tpu as pltpu

F32 = jnp.float32
BF16 = jnp.bfloat16

D_MODEL = 2048
CTX_LEN = 256
GRID_W = 64
EPS = 1e-6
HEAD_DIM = 128
N_Q_HEADS = 12
N_KV_HEADS = 4
Q_GROUP = N_Q_HEADS // N_KV_HEADS
ATTN_W = N_Q_HEADS * HEAD_DIM
KV_W = N_KV_HEADS * HEAD_DIM
ROPE_THETA = 10000.0
POOL_WINDOWS = (2, 4, 8, 16)
POOL_W = 512
POOL_GROUP = 128
FFT_W = 1024
N_FFT_GROUPS = 4
FFT_GROUP = 256
CONV_W = 1024
HALO = 8
V7X_VMEM_BYTES = 64 * 1024 * 1024
VMEM_LIMIT = V7X_VMEM_BYTES - 8 * 1024 * 1024

QK_SCALE_LOG2 = HEAD_DIM ** -0.5 * math.log2(math.e)
SAFE_LOGIT_BOUND = 80.0

NT_DIMS = (((1,), (1,)), ((), ()))


def _params(semantics):
    return pltpu.CompilerParams(dimension_semantics=semantics, vmem_limit_bytes=VMEM_LIMIT)


def _dot(a, b):
    return jnp.dot(a, b, preferred_element_type=F32)


def _rms(x, g):
    ms = jnp.mean(x * x, axis=-1, keepdims=True)
    return x * lax.rsqrt(ms + EPS) * g


def _silu(x):
    return x / (1.0 + jnp.exp(-x))


def _resident(shape):
    zeros = (0,) * len(shape)
    return pl.BlockSpec(shape, lambda *_: zeros, pipeline_mode=pl.Buffered(1))


def _mod_kernel(c_ref, w_ref, b_ref, o_ref):
    s = _silu(c_ref[...]).astype(BF16)
    o_ref[...] = _dot(s, w_ref[...].astype(BF16)) + b_ref[...]


def _modulation(cvec, w_mod, b_mod):
    depth, d, n_out = w_mod.shape
    tn = 1024
    return pl.pallas_call(
        _mod_kernel,
        grid=(depth, n_out // tn),
        in_specs=[
            pl.BlockSpec((8, d), lambda l, j: (0, 0)),
            pl.BlockSpec((None, d, tn), lambda l, j: (l, 0, j)),
            pl.BlockSpec((None, 1, tn), lambda l, j: (l, 0, j)),
        ],
        out_specs=pl.BlockSpec((None, 8, tn), lambda l, j: (l, 0, j)),
        out_shape=jax.ShapeDtypeStruct((depth, 8, n_out), F32),
        compiler_params=_params(("arbitrary", "arbitrary")),
        name="adaln_modulation",
    )(cvec, w_mod, b_mod.reshape(depth, 1, n_out))


def _in_even_kernel(x_ref, sh_ref, sc_ref, ng_ref, w_ref, wvt_ref, qg_ref, kg_ref, cos_ref, sin_ref,
                    q_ref, k_ref, vt_ref, u_ref):
    h = _rms(x_ref[...], ng_ref[...]) * (1.0 + sc_ref[...]) + sh_ref[...]
    hb = h.astype(BF16)
    cosf = cos_ref[...]
    sinf = sin_ref[...]
    lane = lax.broadcasted_iota(jnp.int32, cosf.shape, 1)
    first_of_pair = (lane & 32) == 0

    def norm_rope(z, g):
        zn = _rms(z, g)
        partner = jnp.where(first_of_pair, pltpu.roll(zn, 96, 1), pltpu.roll(zn, 32, 1))
        return zn * cosf + partner * sinf

    qg = qg_ref[...]
    kg = kg_ref[...]
    heads_per_chunk = 4
    cw = heads_per_chunk * HEAD_DIM
    for c in range(ATTN_W // cw):
        z = _dot(hb, w_ref[:, c * cw:(c + 1) * cw])
        for j in range(heads_per_chunk):
            col = c * cw + j * HEAD_DIM
            zh = norm_rope(z[:, j * HEAD_DIM:(j + 1) * HEAD_DIM], qg) * QK_SCALE_LOG2
            q_ref[:, col:col + HEAD_DIM] = zh.astype(BF16)
    z = _dot(hb, w_ref[:, ATTN_W:ATTN_W + KV_W])
    for j in range(N_KV_HEADS):
        zh = norm_rope(z[:, j * HEAD_DIM:(j + 1) * HEAD_DIM], kg)
        k_ref[:, j * HEAD_DIM:(j + 1) * HEAD_DIM] = zh.astype(BF16)
    u_ref[...] = _dot(hb, w_ref[:, ATTN_W + 2 * KV_W:])
    vt_ref[...] = lax.dot_general(wvt_ref[...], hb, NT_DIMS, preferred_element_type=F32).astype(BF16)


def _in_proj_even(x, shift, scale, ng, w_qku, w_vt, qg, kg, cos_t, sin_t, tm):
    b, n, d = x.shape
    row = lambda bi, i: (bi, i, 0)
    vec = lambda bi, i: (bi, 0, 0)
    return pl.pallas_call(
        _in_even_kernel,
        grid=(b, n // tm),
        in_specs=[
            pl.BlockSpec((None, tm, d), row),
            pl.BlockSpec((None, 1, d), vec),
            pl.BlockSpec((None, 1, d), vec),
            _resident((1, d)),
            _resident(w_qku.shape),
            _resident(w_vt.shape),
            _resident((1, HEAD_DIM)),
            _resident((1, HEAD_DIM)),
            pl.BlockSpec((tm, HEAD_DIM), lambda bi, i: (i, 0)),
            pl.BlockSpec((tm, HEAD_DIM), lambda bi, i: (i, 0)),
        ],
        out_specs=[
            pl.BlockSpec((None, tm, ATTN_W), row),
            pl.BlockSpec((None, tm, KV_W), row),
            pl.BlockSpec((None, KV_W, tm), lambda bi, i: (bi, 0, i)),
            pl.BlockSpec((None, tm, POOL_W), row),
        ],
        out_shape=[
            jax.ShapeDtypeStruct((b, n, ATTN_W), BF16),
            jax.ShapeDtypeStruct((b, n, KV_W), BF16),
            jax.ShapeDtypeStruct((b, KV_W, n), BF16),
            jax.ShapeDtypeStruct((b, n, POOL_W), F32),
        ],
        compiler_params=_params(("arbitrary", "arbitrary")),
        name="in_proj_even",
    )(x, shift, scale, ng, w_qku, w_vt, qg, kg, cos_t, sin_t)


def _flash_kernel(q_ref, k_ref, vt_ref, kc_ref, vct_ref, o_ref, acc_ref, l_ref, m_ref, *, tk, stabilize):
    tq = q_ref.shape[0]
    n_keys = k_ref.shape[0]
    q3 = q_ref[...]
    qcat = jnp.concatenate([q3[:, g * HEAD_DIM:(g + 1) * HEAD_DIM] for g in range(Q_GROUP)], axis=0)

    def step(kt, vtt, first):
        s = lax.dot_general(kt, qcat, NT_DIMS, preferred_element_type=F32)
        if stabilize:
            mx = jnp.max(s, axis=0, keepdims=True)
            m_new = mx if first else jnp.maximum(m_ref[...], mx)
            p = jnp.exp2(s - m_new)
        else:
            p = jnp.exp2(s)
        ps = jnp.sum(p, axis=0, keepdims=True)
        pv = _dot(vtt, p.astype(BF16))
        if first:
            acc_ref[...] = pv
            l_ref[...] = ps
        elif stabilize:
            alpha = jnp.exp2(m_ref[...] - m_new)
            acc_ref[...] = acc_ref[...] * alpha + pv
            l_ref[...] = l_ref[...] * alpha + ps
        else:
            acc_ref[...] += pv
            l_ref[...] += ps
        if stabilize:
            m_ref[...] = m_new

    step(kc_ref[...], vct_ref[...], True)

    def body(j, carry):
        off = pl.multiple_of(j * tk, tk)
        step(k_ref[pl.ds(off, tk), :], vt_ref[:, pl.ds(off, tk)], False)
        return carry

    lax.fori_loop(0, n_keys // tk, body, 0)
    o = acc_ref[...] / l_ref[...]
    for g in range(Q_GROUP):
        o_ref[:, g * HEAD_DIM:(g + 1) * HEAD_DIM] = o[:, g * tq:(g + 1) * tq].T.astype(BF16)


def _attention(q, k, vt, kc, vct, *, tq, tk, stabilize):
    b, n, _ = q.shape
    lc = kc.shape[1]
    gw = Q_GROUP * HEAD_DIM
    r = Q_GROUP * tq
    return pl.pallas_call(
        functools.partial(_flash_kernel, tk=tk, stabilize=stabilize),
        grid=(b, N_KV_HEADS, n // tq),
        in_specs=[
            pl.BlockSpec((None, tq, gw), lambda bi, h, i: (bi, i, h)),
            pl.BlockSpec((None, n, HEAD_DIM), lambda bi, h, i: (bi, 0, h), pipeline_mode=pl.Buffered(1)),
            pl.BlockSpec((None, HEAD_DIM, n), lambda bi, h, i: (bi, h, 0), pipeline_mode=pl.Buffered(1)),
            pl.BlockSpec((None, lc, HEAD_DIM), lambda bi, h, i: (bi, 0, h)),
            pl.BlockSpec((None, HEAD_DIM, lc), lambda bi, h, i: (bi, h, 0)),
        ],
        out_specs=pl.BlockSpec((None, tq, gw), lambda bi, h, i: (bi, i, h)),
        out_shape=jax.ShapeDtypeStruct((b, n, ATTN_W), BF16),
        scratch_shapes=[
            pltpu.VMEM((HEAD_DIM, r), F32),
            pltpu.VMEM((1, r), F32),
            pltpu.VMEM((1, r), F32),
        ],
        compiler_params=_params(("arbitrary", "arbitrary", "arbitrary")),
        name="attention_stable" if stabilize else "attention",
    )(q, k, vt, kc, vct)


def _fill_halo(ext_ref, prev_ref, cur_ref, next_ref, i, n_tiles):
    tm = cur_ref.shape[0]
    ext_ref[0:HALO, :] = jnp.where(i > 0, prev_ref[...], 0.0)
    ext_ref[HALO:HALO + tm, :] = cur_ref[...]
    ext_ref[HALO + tm:, :] = jnp.where(i < n_tiles - 1, next_ref[...], 0.0)


def _resid_norm(x, delta, gate, ng, shift, scale):
    x_new = x + gate * delta
    return x_new, _rms(x_new, ng) * (1.0 + scale) + shift


def _out_even_kernel(o_ref, uprev_ref, u_ref, unext_ref, wpool_ref, pscale_ref, wout_ref,
                     x_ref, gate_ref, ng_ref, sh_ref, sc_ref,
                     xo_ref, ho_ref, ext_ref, *, seq_len):
    i = pl.program_id(1)
    tm = u_ref.shape[0]
    _fill_halo(ext_ref, uprev_ref, u_ref, unext_ref, i, pl.num_programs(1))
    t = i * tm + lax.broadcasted_iota(jnp.int32, (tm, POOL_GROUP), 0)
    mix = [o_ref[...]]
    for g, w in enumerate(POOL_WINDOWS):
        cols = slice(g * POOL_GROUP, (g + 1) * POOL_GROUP)
        win = ext_ref[HALO - w // 2:HALO - w // 2 + tm, cols]
        for dlt in range(-w // 2 + 1, w // 2):
            win = win + ext_ref[HALO + dlt:HALO + dlt + tm, cols]
        lo = jnp.clip(t - w // 2, 0, seq_len)
        hi = jnp.clip(t - w // 2 + w, 0, seq_len)
        pooled = win / (hi - lo).astype(F32) - u_ref[:, cols]
        y = _dot(pooled.astype(BF16), wpool_ref[g]) * pscale_ref[:, cols]
        mix.append(y.astype(BF16))
    delta = _dot(jnp.concatenate(mix, axis=1), wout_ref[...])
    x_new, h_new = _resid_norm(x_ref[...], delta, gate_ref[...], ng_ref[...], sh_ref[...], sc_ref[...])
    xo_ref[...] = x_new
    ho_ref[...] = h_new.astype(BF16)


def _out_odd_kernel(y_ref, cprev_ref, c_ref, cnext_ref, bg_ref, cw_ref, wout_ref,
                    x_ref, gate_ref, ng_ref, sh_ref, sc_ref,
                    xo_ref, ho_ref, ext_ref):
    i = pl.program_id(1)
    tm = c_ref.shape[0]
    _fill_halo(ext_ref, cprev_ref, c_ref, cnext_ref, i, pl.num_programs(1))
    conv = (ext_ref[HALO - 1:HALO - 1 + tm, :] * cw_ref[0:1, :]
            + ext_ref[HALO:HALO + tm, :] * cw_ref[1:2, :]
            + ext_ref[HALO + 1:HALO + 1 + tm, :] * cw_ref[2:3, :])
    gated = (bg_ref[...] * conv).astype(BF16)
    delta = _dot(jnp.concatenate([y_ref[...].astype(BF16), gated], axis=1), wout_ref[...])
    x_new, h_new = _resid_norm(x_ref[...], delta, gate_ref[...], ng_ref[...], sh_ref[...], sc_ref[...])
    xo_ref[...] = x_new
    ho_ref[...] = h_new.astype(BF16)


def _halo_specs(width, tm, n):
    tiles8 = tm // HALO
    last8 = n // HALO - 1
    return [
        pl.BlockSpec((None, HALO, width), lambda bi, i: (bi, jnp.maximum(i * tiles8 - 1, 0), 0)),
        pl.BlockSpec((None, tm, width), lambda bi, i: (bi, i, 0)),
        pl.BlockSpec((None, HALO, width), lambda bi, i: (bi, jnp.minimum((i + 1) * tiles8, last8), 0)),
    ]


def _resid_specs(tm, d):
    row = pl.BlockSpec((None, tm, d), lambda bi, i: (bi, i, 0))
    vec = pl.BlockSpec((None, 1, d), lambda bi, i: (bi, 0, 0))
    return [row, vec, _resident((1, d)), vec, vec]


def _resid_outs(b, n, d, tm):
    row = lambda bi, i: (bi, i, 0)
    specs = [pl.BlockSpec((None, tm, d), row), pl.BlockSpec((None, tm, d), row)]
    shapes = [jax.ShapeDtypeStruct((b, n, d), F32), jax.ShapeDtypeStruct((b, n, d), BF16)]
    return specs, shapes


def _out_proj_even(o, u, w_pool, pool_scale, w_out, x, gate, ng, shift, scale, tm):
    b, n, d = x.shape
    out_specs, out_shape = _resid_outs(b, n, d, tm)
    return pl.pallas_call(
        functools.partial(_out_even_kernel, seq_len=n),
        grid=(b, n // tm),
        in_specs=[pl.BlockSpec((None, tm, ATTN_W), lambda bi, i: (bi, i, 0))]
        + _halo_specs(POOL_W, tm, n)
        + [_resident(w_pool.shape), _resident((1, POOL_W)), _resident(w_out.shape)]
        + _resid_specs(tm, d),
        out_specs=out_specs,
        out_shape=out_shape,
        scratch_shapes=[pltpu.VMEM((tm + 2 * HALO, POOL_W), F32)],
        compiler_params=_params(("arbitrary", "arbitrary")),
        name="out_proj_even",
    )(o, u, u, u, w_pool, pool_scale, w_out, x, gate, ng, shift, scale)


def _out_proj_odd(y, cu, bg, conv_w, w_out, x, gate, ng, shift, scale, tm):
    b, n, d = x.shape
    out_specs, out_shape = _resid_outs(b, n, d, tm)
    row = lambda bi, i: (bi, i, 0)
    return pl.pallas_call(
        _out_odd_kernel,
        grid=(b, n // tm),
        in_specs=[pl.BlockSpec((None, tm, FFT_W), row)]
        + _halo_specs(CONV_W, tm, n)
        + [pl.BlockSpec((None, tm, CONV_W), row), _resident(conv_w.shape), _resident(w_out.shape)]
        + _resid_specs(tm, d),
        out_specs=out_specs,
        out_shape=out_shape,
        scratch_shapes=[pltpu.VMEM((tm + 2 * HALO, CONV_W), F32)],
        compiler_params=_params(("arbitrary", "arbitrary")),
        name="out_proj_odd",
    )(y, cu, cu, cu, bg, conv_w, w_out, x, gate, ng, shift, scale)


def _ffn_up_kernel(h_ref, wg_ref, wu_ref, a_ref, wgb_ref, wub_ref):
    @pl.when(pl.program_id(1) == 0)
    def _():
        wgb_ref[...] = wg_ref[...].astype(BF16)
        wub_ref[...] = wu_ref[...].astype(BF16)

    h = h_ref[...]
    a_ref[...] = (_silu(_dot(h, wgb_ref[...])) * _dot(h, wub_ref[...])).astype(BF16)


def _ffn_up(h, wg, wu, layer, *, tm, tn):
    rows, d = h.shape
    d_ff = wg.shape[2]
    w_spec = pl.BlockSpec((None, d, tn), lambda j, i: (layer, 0, j))
    return pl.pallas_call(
        _ffn_up_kernel,
        grid=(d_ff // tn, rows // tm),
        in_specs=[pl.BlockSpec((tm, d), lambda j, i: (i, 0)), w_spec, w_spec],
        out_specs=pl.BlockSpec((tm, tn), lambda j, i: (i, j)),
        out_shape=jax.ShapeDtypeStruct((rows, d_ff), BF16),
        scratch_shapes=[pltpu.VMEM((d, tn), BF16), pltpu.VMEM((d, tn), BF16)],
        compiler_params=_params(("arbitrary", "arbitrary")),
        name="ffn_up",
    )(h, wg, wu)


def _ffn_down_kernel(a_ref, wd_ref, x_ref, gate_ref, ng_ref, sh_ref, sc_ref, *out_refs):
    x_new, h_new = _resid_norm(x_ref[...], _dot(a_ref[...], wd_ref[...]), gate_ref[...], ng_ref[...],
                               sh_ref[...], sc_ref[...])
    if len(out_refs) == 2:
        out_refs[0][...] = x_new
    out_refs[-1][...] = h_new.astype(out_refs[-1].dtype)


def _ffn_down(a, wd, x, gate, ng, shift, scale, *, tm, emit_x):
    b, n, d = x.shape
    d_ff = wd.shape[0]
    row = lambda bi, i: (bi, i, 0)
    out_specs = [pl.BlockSpec((None, tm, d), row)]
    out_shape = [jax.ShapeDtypeStruct((b, n, d), BF16 if emit_x else F32)]
    if emit_x:
        out_specs = [pl.BlockSpec((None, tm, d), row)] + out_specs
        out_shape = [jax.ShapeDtypeStruct((b, n, d), F32)] + out_shape
    return pl.pallas_call(
        _ffn_down_kernel,
        grid=(b, n // tm),
        in_specs=[pl.BlockSpec((None, tm, d_ff), row), _resident(wd.shape)] + _resid_specs(tm, d),
        out_specs=out_specs,
        out_shape=out_shape,
        compiler_params=_params(("arbitrary", "arbitrary")),
        name="ffn_down" if emit_x else "ffn_down_final",
    )(a, wd, x, gate, ng, shift, scale)


def _ffn(h, wg, wu, layer, wd, x, gate, ng, shift, scale, *, emit_x):
    b, n, d = x.shape
    a = _ffn_up(h.reshape(b * n, d), wg, wu, layer, tm=1024, tn=512)
    return _ffn_down(a.reshape(b, n, -1), wd, x, gate, ng, shift, scale, tm=256, emit_x=emit_x)


def _in_odd_kernel(h_ref, w_ref, cs_ref, zr_ref, zi_ref, cu_ref, bg_ref):
    h = h_ref[...]
    cw = 512
    for c in range(FFT_W // cw):
        f = _dot(h, w_ref[:, c * cw:(c + 1) * cw]).astype(BF16)
        for g in range(cw // FFT_GROUP):
            cols = slice(c * cw + g * FFT_GROUP, c * cw + (g + 1) * FFT_GROUP)
            z = _dot(f[:, g * FFT_GROUP:(g + 1) * FFT_GROUP], cs_ref[...])
            zr_ref[:, cols] = z[:, :FFT_GROUP].astype(BF16)
            zi_ref[:, cols] = z[:, FFT_GROUP:].astype(BF16)
    for c in range(CONV_W // cw):
        lo = c * cw
        bg_ref[:, lo:lo + cw] = _dot(h, w_ref[:, FFT_W + lo:FFT_W + lo + cw])
        cg = _dot(h, w_ref[:, FFT_W + CONV_W + lo:FFT_W + CONV_W + lo + cw])
        xin = _dot(h, w_ref[:, FFT_W + 2 * CONV_W + lo:FFT_W + 2 * CONV_W + lo + cw])
        cu_ref[:, lo:lo + cw] = cg * xin


def _in_proj_odd(h, w_in, cs, tm):
    b, n, d = h.shape
    row = lambda bi, i: (bi, i, 0)
    return pl.pallas_call(
        _in_odd_kernel,
        grid=(b, n // tm),
        in_specs=[pl.BlockSpec((None, tm, d), row), _resident(w_in.shape), _resident(cs.shape)],
        out_specs=[pl.BlockSpec((None, tm, FFT_W), row)] * 2 + [pl.BlockSpec((None, tm, CONV_W), row)] * 2,
        out_shape=[jax.ShapeDtypeStruct((b, n, FFT_W), BF16)] * 2
        + [jax.ShapeDtypeStruct((b, n, CONV_W), F32)] * 2,
        compiler_params=_params(("arbitrary", "arbitrary")),
        name="in_proj_odd",
    )(h, w_in, cs)


def _dft1_kernel(zr_ref, zi_ref, w_ref, tr_ref, ti_ref):
    zr = pltpu.einshape("mjc->jmc", zr_ref[...])
    zi = pltpu.einshape("mjc->jmc", zi_ref[...])
    half = zr_ref.shape[0]
    trs, tis = [], []
    for j in range(zr.shape[0]):
        t = _dot(w_ref[...], jnp.concatenate([zr[j], zi[j]], axis=0))
        trs.append(t[:half].astype(BF16))
        tis.append(t[half:].astype(BF16))
    tr_ref[...] = pltpu.einshape("jmc->mjc", jnp.stack(trs, axis=0))
    ti_ref[...] = pltpu.einshape("jmc->mjc", jnp.stack(tis, axis=0))


def _dft_stage1(zr, zi, w1, jb, ct):
    b, n1, n2, ch = zr.shape
    blk = pl.BlockSpec((None, n1, jb, ct), lambda bi, j, c: (bi, 0, j, c))
    return pl.pallas_call(
        _dft1_kernel,
        grid=(b, n2 // jb, ch // ct),
        in_specs=[blk, blk, _resident(w1.shape)],
        out_specs=[blk, blk],
        out_shape=[jax.ShapeDtypeStruct(zr.shape, BF16)] * 2,
        compiler_params=_params(("arbitrary", "arbitrary", "arbitrary")),
        name="dft_stage1",
    )(zr, zi, w1)


def _dft2_kernel(tr_ref, ti_ref, er_ref, ei_ref, y_ref):
    j_of_row = lax.broadcasted_iota(jnp.int32, er_ref.shape, 0) % HALO

    def block_diag(e_ref):
        e = e_ref[...]
        return jnp.concatenate([jnp.where(j_of_row == j, e, 0.0) for j in range(HALO)], axis=1).astype(BF16)

    y = _dot(block_diag(er_ref), tr_ref[...]) + _dot(block_diag(ei_ref), ti_ref[...])
    y_ref[...] = y.reshape(y_ref.shape)


def _dft_stage2(tr, ti, er, ei):
    b, n, ch = tr.shape
    n_blk, rows, n2 = er.shape
    k2 = rows // HALO
    t_spec = pl.BlockSpec((None, rows, ch), lambda bi, j: (bi, j, 0))
    g_spec = pl.BlockSpec((None, rows, n2), lambda bi, j: (j, 0, 0))
    return pl.pallas_call(
        _dft2_kernel,
        grid=(b, n_blk),
        in_specs=[t_spec, t_spec, g_spec, g_spec],
        out_specs=pl.BlockSpec((None, k2, HALO, ch), lambda bi, j: (bi, 0, j, 0)),
        out_shape=jax.ShapeDtypeStruct((b, k2, n // k2, ch), F32),
        compiler_params=_params(("arbitrary", "arbitrary")),
        name="dft_stage2",
    )(tr, ti, er, ei)


def _rope_tables(n):
    n_rows = n // GRID_W
    row = jnp.repeat(jnp.arange(n_rows, dtype=F32), GRID_W)
    col = jnp.tile(jnp.arange(GRID_W, dtype=F32), n_rows)
    quarter = HEAD_DIM // 4
    freqs = ROPE_THETA ** (-jnp.arange(quarter, dtype=F32) / quarter)
    ar = row[:, None] * freqs
    ac = col[:, None] * freqs
    cos_t = jnp.concatenate([jnp.cos(ar), jnp.cos(ar), jnp.cos(ac), jnp.cos(ac)], axis=-1)
    sin_t = jnp.concatenate([-jnp.sin(ar), jnp.sin(ar), -jnp.sin(ac), jnp.sin(ac)], axis=-1)
    return cos_t, sin_t


def _unit_angle(num, den):
    return (2.0 * math.pi / den) * (num % den).astype(F32)


def _dft_tables(n):
    n1 = 128
    n2 = n // n1
    ch = jnp.arange(FFT_GROUP, dtype=jnp.int32)
    a = _unit_angle(ch[:, None] * ch[None, :], FFT_GROUP)
    cs = jnp.concatenate([jnp.cos(a), -jnp.sin(a)], axis=1).astype(BF16)
    k1 = jnp.arange(n1, dtype=jnp.int32)
    a1 = _unit_angle(k1[:, None] * k1[None, :], n1)
    fr, fi = jnp.cos(a1), -jnp.sin(a1)
    w1 = jnp.concatenate([jnp.concatenate([fr, -fi], axis=1),
                          jnp.concatenate([fi, fr], axis=1)], axis=0).astype(BF16)
    m = jnp.arange(n2, dtype=jnp.int32)
    aa = _unit_angle(k1[:, None] * m[None, :], n // n1)[:, None, :]
    ab = _unit_angle(k1[:, None] * m[None, :], n)[None, :, :]
    norm = 1.0 / math.sqrt(n * FFT_GROUP)
    cos2 = (jnp.cos(aa) * jnp.cos(ab) - jnp.sin(aa) * jnp.sin(ab)) * norm
    sin2 = (jnp.sin(aa) * jnp.cos(ab) + jnp.cos(aa) * jnp.sin(ab)) * norm

    def by_block(e):
        e = e.reshape(n // n1, n1 // HALO, HALO, n2).transpose(1, 0, 2, 3)
        return e.reshape(n1 // HALO, (n // n1) * HALO, n2)

    return cs, w1, by_block(cos2), by_block(sin2)


def kernel(x, c, ctx, c_ctx, norm_g, w_mod, b_mod, w_in_even, w_out_even, q_norm_g, k_norm_g, w_pool,
           pool_scale, w_in_odd, w_out_odd, conv_w, w_ffn_gate, w_ffn_up, w_ffn_down, final_g):
    b, n, d = x.shape
    tm = 512

    cvec = jnp.zeros((8, d), F32).at[:b].set(c).at[b].set(c_ctx)
    mod = _modulation(cvec, w_mod, b_mod).reshape(w_mod.shape[0], 8, 6, 1, d)

    def mvec(layer, comp, rows=slice(0, b)):
        return mod[layer, rows, comp]

    zeros_vec = jnp.zeros((b, 1, d), F32)

    w0 = w_in_even[0]
    w_qku = w0.astype(BF16)
    w_vt = w0[:, ATTN_W + KV_W:ATTN_W + 2 * KV_W].T.astype(BF16)
    qg = q_norm_g[0].reshape(1, HEAD_DIM)
    kg = k_norm_g[0].reshape(1, HEAD_DIM)
    ng00 = norm_g[0, 0].reshape(1, d)
    cos_t, sin_t = _rope_tables(n)
    q, k, vt, u = _in_proj_even(x, mvec(0, 0), mvec(0, 1), ng00, w_qku, w_vt, qg, kg, cos_t, sin_t, tm)

    lc = ctx.shape[1]
    ctx_rows = slice(b, b + 1)
    ctx_shift = jnp.broadcast_to(mvec(0, 0, ctx_rows), (b, 1, d))
    ctx_scale = jnp.broadcast_to(mvec(0, 1, ctx_rows), (b, 1, d))
    ident_cos = jnp.ones((lc, HEAD_DIM), F32)
    ident_sin = jnp.zeros((lc, HEAD_DIM), F32)
    _, kc, vct, _ = _in_proj_even(ctx, ctx_shift, ctx_scale, ng00, w_qku, w_vt, qg, kg,
                                  ident_cos, ident_sin, lc)

    logit_bound = (math.sqrt(HEAD_DIM) * math.log2(math.e) * 1.02
                   * jnp.max(jnp.abs(q_norm_g[0])) * jnp.max(jnp.abs(k_norm_g[0])))
    attn = functools.partial(_attention, q, k, vt, kc, vct, tq=512, tk=2048)
    o = lax.cond(logit_bound <= SAFE_LOGIT_BOUND,
                 lambda: attn(stabilize=False), lambda: attn(stabilize=True))

    x1, h1 = _out_proj_even(o, u, w_pool[0].astype(BF16), pool_scale[0].reshape(1, POOL_W),
                            w_out_even[0].astype(BF16), x, mvec(0, 2), norm_g[0, 1].reshape(1, d),
                            mvec(0, 3), mvec(0, 4), tm)
    x2, h2 = _ffn(h1, w_ffn_gate, w_ffn_up, 0, w_ffn_down[0].astype(BF16),
                  x1, mvec(0, 5), norm_g[1, 0].reshape(1, d), mvec(1, 0), mvec(1, 1),
                  emit_x=True)

    cs, w1, gr, gi = _dft_tables(n)
    zr, zi, cu, bg = _in_proj_odd(h2, w_in_odd[0].astype(BF16), cs, tm)
    n1 = 128
    tr, ti = _dft_stage1(zr.reshape(b, n1, n // n1, FFT_W), zi.reshape(b, n1, n // n1, FFT_W), w1, 16, 512)
    y = _dft_stage2(tr.reshape(b, n, FFT_W), ti.reshape(b, n, FFT_W), gr, gi).reshape(b, n, FFT_W)
    x3, h3 = _out_proj_odd(y, cu, bg, conv_w[0], w_out_odd[0].astype(BF16), x2, mvec(1, 2),
                           norm_g[1, 1].reshape(1, d), mvec(1, 3), mvec(1, 4), tm)
    (out,) = _ffn(h3, w_ffn_gate, w_ffn_up, 1, w_ffn_down[1].astype(BF16),
                  x3, mvec(1, 5), final_g.reshape(1, d), zeros_vec, zeros_vec,
                  emit_x=False)
    return out
```

```python
import functools
import math
from typing import NamedTuple

import jax
import jax.numpy as jnp
from jax import lax
from jax.experimental import pallas as pl
from jax.experimental.pallas import tpu as pltpu

F32 = jnp.float32
BF16 = jnp.bfloat16

D_MODEL = 2048
CTX_LEN = 256
GRID_W = 64
EPS = 1e-6
HEAD_DIM = 128
N_Q_HEADS = 12
N_KV_HEADS = 4
Q_GROUP = N_Q_HEADS // N_KV_HEADS
ATTN_W = N_Q_HEADS * HEAD_DIM
KV_W = N_KV_HEADS * HEAD_DIM
ROPE_THETA = 10000.0
POOL_WINDOWS = (2, 4, 8, 16)
POOL_W = 512
POOL_GROUP = 128
FFT_W = 1024
N_FFT_GROUPS = 4
FFT_GROUP = 256
CONV_W = 1024
HALO = 8
V7X_VMEM_BYTES = 64 * 1024 * 1024
VMEM_LIMIT = V7X_VMEM_BYTES - 8 * 1024 * 1024

QK_SCALE_LOG2 = HEAD_DIM ** -0.5 * math.log2(math.e)
SAFE_LOGIT_BOUND = 80.0

NT_DIMS = (((1,), (1,)), ((), ()))


class Tiles(NamedTuple):
    proj_rows: int = 512
    attn_queries: int = 512
    attn_keys: int = 2048
    ffn_up_rows: int = 1024
    ffn_up_cols: int = 512
    ffn_down_rows: int = 256
    mod_cols: int = 1024
    dft_n2: int = 16
    dft_ch: int = 512


TILES = Tiles()


def _params(semantics):
    return pltpu.CompilerParams(dimension_semantics=semantics, vmem_limit_bytes=VMEM_LIMIT)


def _dot(a, b):
    return jnp.dot(a, b, preferred_element_type=F32)


def _rms(x, g):
    ms = jnp.mean(x * x, axis=-1, keepdims=True)
    return x * lax.rsqrt(ms + EPS) * g


def _silu(x):
    return x / (1.0 + jnp.exp(-x))


def _resident(shape):
    zeros = (0,) * len(shape)
    return pl.BlockSpec(shape, lambda *_: zeros, pipeline_mode=pl.Buffered(1))


def _mod_kernel(c_ref, w_ref, b_ref, o_ref):
    s = _silu(c_ref[...]).astype(BF16)
    o_ref[...] = _dot(s, w_ref[...].astype(BF16)) + b_ref[...]


def _modulation(cvec, w_mod, b_mod):
    depth, d, n_out = w_mod.shape
    tn = TILES.mod_cols
    return pl.pallas_call(
        _mod_kernel,
        grid=(depth, n_out // tn),
        in_specs=[
            pl.BlockSpec((8, d), lambda l, j: (0, 0)),
            pl.BlockSpec((None, d, tn), lambda l, j: (l, 0, j)),
            pl.BlockSpec((None, 1, tn), lambda l, j: (l, 0, j)),
        ],
        out_specs=pl.BlockSpec((None, 8, tn), lambda l, j: (l, 0, j)),
        out_shape=jax.ShapeDtypeStruct((depth, 8, n_out), F32),
        compiler_params=_params(("arbitrary", "arbitrary")),
        name="adaln_modulation",
    )(cvec, w_mod, b_mod.reshape(depth, 1, n_out))


def _in_even_kernel(x_ref, sh_ref, sc_ref, ng_ref, w_ref, wvt_ref, qg_ref, kg_ref, cos_ref, sin_ref,
                    q_ref, k_ref, vt_ref, u_ref):
    h = _rms(x_ref[...], ng_ref[...]) * (1.0 + sc_ref[...]) + sh_ref[...]
    hb = h.astype(BF16)
    cosf = cos_ref[...]
    sinf = sin_ref[...]
    lane = lax.broadcasted_iota(jnp.int32, cosf.shape, 1)
    first_of_pair = (lane & 32) == 0

    def norm_rope(z, g):
        zn = _rms(z, g)
        partner = jnp.where(first_of_pair, pltpu.roll(zn, 96, 1), pltpu.roll(zn, 32, 1))
        return zn * cosf + partner * sinf

    qg = qg_ref[...]
    kg = kg_ref[...]
    heads_per_chunk = 4
    cw = heads_per_chunk * HEAD_DIM
    for c in range(ATTN_W // cw):
        z = _dot(hb, w_ref[:, c * cw:(c + 1) * cw])
        for j in range(heads_per_chunk):
            col = c * cw + j * HEAD_DIM
            zh = norm_rope(z[:, j * HEAD_DIM:(j + 1) * HEAD_DIM], qg) * QK_SCALE_LOG2
            q_ref[:, col:col + HEAD_DIM] = zh.astype(BF16)
    z = _dot(hb, w_ref[:, ATTN_W:ATTN_W + KV_W])
    for j in range(N_KV_HEADS):
        zh = norm_rope(z[:, j * HEAD_DIM:(j + 1) * HEAD_DIM], kg)
        k_ref[:, j * HEAD_DIM:(j + 1) * HEAD_DIM] = zh.astype(BF16)
    u_ref[...] = _dot(hb, w_ref[:, ATTN_W + 2 * KV_W:])
    vt_ref[...] = lax.dot_general(wvt_ref[...], hb, NT_DIMS, preferred_element_type=F32).astype(BF16)


def _in_proj_even(x, shift, scale, ng, w_qku, w_vt, qg, kg, cos_t, sin_t, tm):
    b, n, d = x.shape
    row = lambda bi, i: (bi, i, 0)
    vec = lambda bi, i: (bi, 0, 0)
    return pl.pallas_call(
        _in_even_kernel,
        grid=(b, n // tm),
        in_specs=[
            pl.BlockSpec((None, tm, d), row),
            pl.BlockSpec((None, 1, d), vec),
            pl.BlockSpec((None, 1, d), vec),
            _resident((1, d)),
            _resident(w_qku.shape),
            _resident(w_vt.shape),
            _resident((1, HEAD_DIM)),
            _resident((1, HEAD_DIM)),
            pl.BlockSpec((tm, HEAD_DIM), lambda bi, i: (i, 0)),
            pl.BlockSpec((tm, HEAD_DIM), lambda bi, i: (i, 0)),
        ],
        out_specs=[
            pl.BlockSpec((None, tm, ATTN_W), row),
            pl.BlockSpec((None, tm, KV_W), row),
            pl.BlockSpec((None, KV_W, tm), lambda bi, i: (bi, 0, i)),
            pl.BlockSpec((None, tm, POOL_W), row),
        ],
        out_shape=[
            jax.ShapeDtypeStruct((b, n, ATTN_W), BF16),
            jax.ShapeDtypeStruct((b, n, KV_W), BF16),
            jax.ShapeDtypeStruct((b, KV_W, n), BF16),
            jax.ShapeDtypeStruct((b, n, POOL_W), F32),
        ],
        compiler_params=_params(("arbitrary", "arbitrary")),
        name="in_proj_even",
    )(x, shift, scale, ng, w_qku, w_vt, qg, kg, cos_t, sin_t)


def _stack_group_queries(q_ref):
    q3 = q_ref[...]
    return jnp.concatenate([q3[:, g * HEAD_DIM:(g + 1) * HEAD_DIM] for g in range(Q_GROUP)], axis=0)


def _write_attention_out(o_ref, acc_ref, l_ref):
    tq = o_ref.shape[0]
    o = acc_ref[...] / l_ref[...]
    for g in range(Q_GROUP):
        o_ref[:, g * HEAD_DIM:(g + 1) * HEAD_DIM] = o[:, g * tq:(g + 1) * tq].T.astype(BF16)


def _flash_pipelined_kernel(q_ref, k_ref, vt_ref, kc_ref, vct_ref, o_ref, acc_ref, l_ref, pa_ref, pb_ref,
                            *, tk):
    n_tiles = k_ref.shape[0] // tk
    qcat = _stack_group_queries(q_ref)

    def probs(kt):
        return jnp.exp2(lax.dot_general(kt, qcat, NT_DIMS, preferred_element_type=F32))

    def offset(tile):
        return tile * tk if isinstance(tile, int) else pl.multiple_of(tile * tk, tk)

    def logits_to(dst_ref, tile):
        p = probs(k_ref[pl.ds(offset(tile), tk), :])
        l_ref[...] += jnp.sum(p, axis=0, keepdims=True)
        dst_ref[...] = p.astype(BF16)

    def values_from(src_ref, tile):
        acc_ref[...] += _dot(vt_ref[:, pl.ds(offset(tile), tk)], src_ref[...])

    pc = probs(kc_ref[...])
    l_ref[...] = jnp.sum(pc, axis=0, keepdims=True)
    acc_ref[...] = _dot(vct_ref[...], pc.astype(BF16))
    logits_to(pa_ref, 0)

    def body(i, carry):
        logits_to(pb_ref, 2 * i + 1)
        values_from(pa_ref, 2 * i)
        logits_to(pa_ref, 2 * i + 2)
        values_from(pb_ref, 2 * i + 1)
        return carry

    lax.fori_loop(0, n_tiles // 2 - 1, body, 0)
    logits_to(pb_ref, n_tiles - 1)
    values_from(pa_ref, n_tiles - 2)
    values_from(pb_ref, n_tiles - 1)
    _write_attention_out(o_ref, acc_ref, l_ref)


def _flash_kernel(q_ref, k_ref, vt_ref, kc_ref, vct_ref, o_ref, acc_ref, l_ref, m_ref, *, tk, stabilize):
    n_keys = k_ref.shape[0]
    qcat = _stack_group_queries(q_ref)

    def step(kt, vtt, first):
        s = lax.dot_general(kt, qcat, NT_DIMS, preferred_element_type=F32)
        if stabilize:
            mx = jnp.max(s, axis=0, keepdims=True)
            m_new = mx if first else jnp.maximum(m_ref[...], mx)
            p = jnp.exp2(s - m_new)
        else:
            p = jnp.exp2(s)
        ps = jnp.sum(p, axis=0, keepdims=True)
        pv = _dot(vtt, p.astype(BF16))
        if first:
            acc_ref[...] = pv
            l_ref[...] = ps
        elif stabilize:
            alpha = jnp.exp2(m_ref[...] - m_new)
            acc_ref[...] = acc_ref[...] * alpha + pv
            l_ref[...] = l_ref[...] * alpha + ps
        else:
            acc_ref[...] += pv
            l_ref[...] += ps
        if stabilize:
            m_ref[...] = m_new

    step(kc_ref[...], vct_ref[...], True)

    def body(j, carry):
        off = pl.multiple_of(j * tk, tk)
        step(k_ref[pl.ds(off, tk), :], vt_ref[:, pl.ds(off, tk)], False)
        return carry

    lax.fori_loop(0, n_keys // tk, body, 0)
    _write_attention_out(o_ref, acc_ref, l_ref)


def _attention(q, k, vt, kc, vct, *, tq, tk, stabilize):
    b, n, _ = q.shape
    lc = kc.shape[1]
    gw = Q_GROUP * HEAD_DIM
    r = Q_GROUP * tq
    if stabilize:
        body = functools.partial(_flash_kernel, tk=tk, stabilize=True)
        scratch = [pltpu.VMEM((1, r), F32)]
    else:
        body = functools.partial(_flash_pipelined_kernel, tk=tk)
        scratch = [pltpu.VMEM((tk, r), BF16), pltpu.VMEM((tk, r), BF16)]
    return pl.pallas_call(
        body,
        grid=(b, N_KV_HEADS, n // tq),
        in_specs=[
            pl.BlockSpec((None, tq, gw), lambda bi, h, i: (bi, i, h)),
            pl.BlockSpec((None, n, HEAD_DIM), lambda bi, h, i: (bi, 0, h), pipeline_mode=pl.Buffered(1)),
            pl.BlockSpec((None, HEAD_DIM, n), lambda bi, h, i: (bi, h, 0), pipeline_mode=pl.Buffered(1)),
            pl.BlockSpec((None, lc, HEAD_DIM), lambda bi, h, i: (bi, 0, h)),
            pl.BlockSpec((None, HEAD_DIM, lc), lambda bi, h, i: (bi, h, 0)),
        ],
        out_specs=pl.BlockSpec((None, tq, gw), lambda bi, h, i: (bi, i, h)),
        out_shape=jax.ShapeDtypeStruct((b, n, ATTN_W), BF16),
        scratch_shapes=[pltpu.VMEM((HEAD_DIM, r), F32), pltpu.VMEM((1, r), F32)] + scratch,
        compiler_params=_params(("arbitrary", "arbitrary", "arbitrary")),
        name="attention_stable" if stabilize else "attention",
    )(q, k, vt, kc, vct)


def _fill_halo(ext_ref, prev_ref, cur_ref, next_ref, i, n_tiles):
    tm = cur_ref.shape[0]
    ext_ref[0:HALO, :] = jnp.where(i > 0, prev_ref[...], 0.0)
    ext_ref[HALO:HALO + tm, :] = cur_ref[...]
    ext_ref[HALO + tm:, :] = jnp.where(i < n_tiles - 1, next_ref[...], 0.0)


def _resid_norm(x, delta, gate, ng, shift, scale):
    x_new = x + gate * delta
    return x_new, _rms(x_new, ng) * (1.0 + scale) + shift


def _out_even_kernel(o_ref, uprev_ref, u_ref, unext_ref, wpool_ref, pscale_ref, wout_ref,
                     x_ref, gate_ref, ng_ref, sh_ref, sc_ref,
                     xo_ref, ho_ref, ext_ref, *, seq_len):
    i = pl.program_id(1)
    tm = u_ref.shape[0]
    _fill_halo(ext_ref, uprev_ref, u_ref, unext_ref, i, pl.num_programs(1))
    t = i * tm + lax.broadcasted_iota(jnp.int32, (tm, POOL_GROUP), 0)
    mix = [o_ref[...]]
    for g, w in enumerate(POOL_WINDOWS):
        cols = slice(g * POOL_GROUP, (g + 1) * POOL_GROUP)
        win = ext_ref[HALO - w // 2:HALO - w // 2 + tm, cols]
        for dlt in range(-w // 2 + 1, w // 2):
            win = win + ext_ref[HALO + dlt:HALO + dlt + tm, cols]
        lo = jnp.clip(t - w // 2, 0, seq_len)
        hi = jnp.clip(t - w // 2 + w, 0, seq_len)
        pooled = win / (hi - lo).astype(F32) - u_ref[:, cols]
        y = _dot(pooled.astype(BF16), wpool_ref[g]) * pscale_ref[:, cols]
        mix.append(y.astype(BF16))
    delta = _dot(jnp.concatenate(mix, axis=1), wout_ref[...])
    x_new, h_new = _resid_norm(x_ref[...], delta, gate_ref[...], ng_ref[...], sh_ref[...], sc_ref[...])
    xo_ref[...] = x_new
    ho_ref[...] = h_new.astype(BF16)


def _out_odd_kernel(y_ref, cprev_ref, c_ref, cnext_ref, bg_ref, cw_ref, wout_ref,
                    x_ref, gate_ref, ng_ref, sh_ref, sc_ref,
                    xo_ref, ho_ref, ext_ref):
    i = pl.program_id(1)
    tm = c_ref.shape[0]
    _fill_halo(ext_ref, cprev_ref, c_ref, cnext_ref, i, pl.num_programs(1))
    halves = [slice(r0, r0 + tm // 2) for r0 in (0, tm // 2)]
    mixes = []
    for rows in halves:
        base = HALO + rows.start
        sub = tm // 2
        conv = (ext_ref[base - 1:base - 1 + sub, :] * cw_ref[0:1, :]
                + ext_ref[base:base + sub, :] * cw_ref[1:2, :]
                + ext_ref[base + 1:base + 1 + sub, :] * cw_ref[2:3, :])
        gated = (bg_ref[rows, :] * conv).astype(BF16)
        mixes.append(jnp.concatenate([y_ref[rows, :].astype(BF16), gated], axis=1))
    deltas = [_dot(mix, wout_ref[...]) for mix in mixes]
    for rows, delta in zip(halves, deltas):
        x_new, h_new = _resid_norm(x_ref[rows, :], delta, gate_ref[...], ng_ref[...], sh_ref[...], sc_ref[...])
        xo_ref[rows, :] = x_new
        ho_ref[rows, :] = h_new.astype(BF16)


def _halo_specs(width, tm, n):
    tiles8 = tm // HALO
    last8 = n // HALO - 1
    return [
        pl.BlockSpec((None, HALO, width), lambda bi, i: (bi, jnp.maximum(i * tiles8 - 1, 0), 0)),
        pl.BlockSpec((None, tm, width), lambda bi, i: (bi, i, 0)),
        pl.BlockSpec((None, HALO, width), lambda bi, i: (bi, jnp.minimum((i + 1) * tiles8, last8), 0)),
    ]


def _resid_specs(tm, d):
    row = pl.BlockSpec((None, tm, d), lambda bi, i: (bi, i, 0))
    vec = pl.BlockSpec((None, 1, d), lambda bi, i: (bi, 0, 0))
    return [row, vec, _resident((1, d)), vec, vec]


def _resid_outs(b, n, d, tm):
    row = lambda bi, i: (bi, i, 0)
    specs = [pl.BlockSpec((None, tm, d), row), pl.BlockSpec((None, tm, d), row)]
    shapes = [jax.ShapeDtypeStruct((b, n, d), F32), jax.ShapeDtypeStruct((b, n, d), BF16)]
    return specs, shapes


def _out_proj_even(o, u, w_pool, pool_scale, w_out, x, gate, ng, shift, scale, tm):
    b, n, d = x.shape
    out_specs, out_shape = _resid_outs(b, n, d, tm)
    return pl.pallas_call(
        functools.partial(_out_even_kernel, seq_len=n),
        grid=(b, n // tm),
        in_specs=[pl.BlockSpec((None, tm, ATTN_W), lambda bi, i: (bi, i, 0))]
        + _halo_specs(POOL_W, tm, n)
        + [_resident(w_pool.shape), _resident((1, POOL_W)), _resident(w_out.shape)]
        + _resid_specs(tm, d),
        out_specs=out_specs,
        out_shape=out_shape,
        scratch_shapes=[pltpu.VMEM((tm + 2 * HALO, POOL_W), F32)],
        compiler_params=_params(("arbitrary", "arbitrary")),
        name="out_proj_even",
    )(o, u, u, u, w_pool, pool_scale, w_out, x, gate, ng, shift, scale)


def _out_proj_odd(y, cu, bg, conv_w, w_out, x, gate, ng, shift, scale, tm):
    b, n, d = x.shape
    out_specs, out_shape = _resid_outs(b, n, d, tm)
    row = lambda bi, i: (bi, i, 0)
    return pl.pallas_call(
        _out_odd_kernel,
        grid=(b, n // tm),
        in_specs=[pl.BlockSpec((None, tm, FFT_W), row)]
        + _halo_specs(CONV_W, tm, n)
        + [pl.BlockSpec((None, tm, CONV_W), row), _resident(conv_w.shape), _resident(w_out.shape)]
        + _resid_specs(tm, d),
        out_specs=out_specs,
        out_shape=out_shape,
        scratch_shapes=[pltpu.VMEM((tm + 2 * HALO, CONV_W), F32)],
        compiler_params=_params(("arbitrary", "arbitrary")),
        name="out_proj_odd",
    )(y, cu, cu, cu, bg, conv_w, w_out, x, gate, ng, shift, scale)


def _ffn_up_kernel(h_ref, wg_ref, wu_ref, a_ref, wgb_ref, wub_ref):
    @pl.when(pl.program_id(1) == 0)
    def _():
        wgb_ref[...] = wg_ref[...].astype(BF16)
        wub_ref[...] = wu_ref[...].astype(BF16)

    h = h_ref[...]
    a_ref[...] = (_silu(_dot(h, wgb_ref[...])) * _dot(h, wub_ref[...])).astype(BF16)


def _ffn_up(h, wg, wu, layer, *, tm, tn):
    rows, d = h.shape
    d_ff = wg.shape[2]
    w_spec = pl.BlockSpec((None, d, tn), lambda j, i: (layer, 0, j))
    return pl.pallas_call(
        _ffn_up_kernel,
        grid=(d_ff // tn, rows // tm),
        in_specs=[pl.BlockSpec((tm, d), lambda j, i: (i, 0)), w_spec, w_spec],
        out_specs=pl.BlockSpec((tm, tn), lambda j, i: (i, j)),
        out_shape=jax.ShapeDtypeStruct((rows, d_ff), BF16),
        scratch_shapes=[pltpu.VMEM((d, tn), BF16), pltpu.VMEM((d, tn), BF16)],
        compiler_params=_params(("arbitrary", "arbitrary")),
        name="ffn_up",
    )(h, wg, wu)


def _ffn_down_kernel(a_ref, wd_ref, x_ref, gate_ref, ng_ref, sh_ref, sc_ref, *out_refs):
    x_new, h_new = _resid_norm(x_ref[...], _dot(a_ref[...], wd_ref[...]), gate_ref[...], ng_ref[...],
                               sh_ref[...], sc_ref[...])
    if len(out_refs) == 2:
        out_refs[0][...] = x_new
    out_refs[-1][...] = h_new.astype(out_refs[-1].dtype)


def _ffn_down(a, wd, x, gate, ng, shift, scale, *, tm, emit_x):
    b, n, d = x.shape
    d_ff = wd.shape[0]
    row = lambda bi, i: (bi, i, 0)
    out_specs = [pl.BlockSpec((None, tm, d), row)]
    out_shape = [jax.ShapeDtypeStruct((b, n, d), BF16 if emit_x else F32)]
    if emit_x:
        out_specs = [pl.BlockSpec((None, tm, d), row)] + out_specs
        out_shape = [jax.ShapeDtypeStruct((b, n, d), F32)] + out_shape
    return pl.pallas_call(
        _ffn_down_kernel,
        grid=(b, n // tm),
        in_specs=[pl.BlockSpec((None, tm, d_ff), row), _resident(wd.shape)] + _resid_specs(tm, d),
        out_specs=out_specs,
        out_shape=out_shape,
        compiler_params=_params(("arbitrary", "arbitrary")),
        name="ffn_down" if emit_x else "ffn_down_final",
    )(a, wd, x, gate, ng, shift, scale)


def _ffn(h, wg, wu, layer, wd, x, gate, ng, shift, scale, *, emit_x):
    b, n, d = x.shape
    a = _ffn_up(h.reshape(b * n, d), wg, wu, layer, tm=TILES.ffn_up_rows, tn=TILES.ffn_up_cols)
    return _ffn_down(a.reshape(b, n, -1), wd, x, gate, ng, shift, scale, tm=TILES.ffn_down_rows,
                     emit_x=emit_x)


def _in_odd_kernel(h_ref, w_ref, cs_ref, zr_ref, zi_ref, cu_ref, bg_ref):
    h = h_ref[...]
    cw = 512
    for c in range(FFT_W // cw):
        f = _dot(h, w_ref[:, c * cw:(c + 1) * cw]).astype(BF16)
        for g in range(cw // FFT_GROUP):
            cols = slice(c * cw + g * FFT_GROUP, c * cw + (g + 1) * FFT_GROUP)
            z = _dot(f[:, g * FFT_GROUP:(g + 1) * FFT_GROUP], cs_ref[...])
            zr_ref[:, cols] = z[:, :FFT_GROUP].astype(BF16)
            zi_ref[:, cols] = z[:, FFT_GROUP:].astype(BF16)
    for c in range(CONV_W // cw):
        lo = c * cw
        bg_ref[:, lo:lo + cw] = _dot(h, w_ref[:, FFT_W + lo:FFT_W + lo + cw])
        cg = _dot(h, w_ref[:, FFT_W + CONV_W + lo:FFT_W + CONV_W + lo + cw])
        xin = _dot(h, w_ref[:, FFT_W + 2 * CONV_W + lo:FFT_W + 2 * CONV_W + lo + cw])
        cu_ref[:, lo:lo + cw] = cg * xin


def _in_proj_odd(h, w_in, cs, tm):
    b, n, d = h.shape
    row = lambda bi, i: (bi, i, 0)
    return pl.pallas_call(
        _in_odd_kernel,
        grid=(b, n // tm),
        in_specs=[pl.BlockSpec((None, tm, d), row), _resident(w_in.shape), _resident(cs.shape)],
        out_specs=[pl.BlockSpec((None, tm, FFT_W), row)] * 2 + [pl.BlockSpec((None, tm, CONV_W), row)] * 2,
        out_shape=[jax.ShapeDtypeStruct((b, n, FFT_W), BF16)] * 2
        + [jax.ShapeDtypeStruct((b, n, CONV_W), F32)] * 2,
        compiler_params=_params(("arbitrary", "arbitrary")),
        name="in_proj_odd",
    )(h, w_in, cs)


def _dft1_kernel(zr_ref, zi_ref, w_ref, tr_ref, ti_ref):
    zr = pltpu.einshape("mjc->jmc", zr_ref[...])
    zi = pltpu.einshape("mjc->jmc", zi_ref[...])
    half = zr_ref.shape[0]
    trs, tis = [], []
    for j in range(zr.shape[0]):
        t = _dot(w_ref[...], jnp.concatenate([zr[j], zi[j]], axis=0))
        trs.append(t[:half].astype(BF16))
        tis.append(t[half:].astype(BF16))
    tr_ref[...] = pltpu.einshape("jmc->mjc", jnp.stack(trs, axis=0))
    ti_ref[...] = pltpu.einshape("jmc->mjc", jnp.stack(tis, axis=0))


def _dft_stage1(zr, zi, w1, jb, ct):
    b, n1, n2, ch = zr.shape
    blk = pl.BlockSpec((None, n1, jb, ct), lambda bi, j, c: (bi, 0, j, c))
    return pl.pallas_call(
        _dft1_kernel,
        grid=(b, n2 // jb, ch // ct),
        in_specs=[blk, blk, _resident(w1.shape)],
        out_specs=[blk, blk],
        out_shape=[jax.ShapeDtypeStruct(zr.shape, BF16)] * 2,
        compiler_params=_params(("arbitrary", "arbitrary", "arbitrary")),
        name="dft_stage1",
    )(zr, zi, w1)


def _dft2_kernel(tr_ref, ti_ref, er_ref, ei_ref, y_ref):
    j_of_row = lax.broadcasted_iota(jnp.int32, er_ref.shape, 0) % HALO

    def block_diag(e_ref):
        e = e_ref[...]
        return jnp.concatenate([jnp.where(j_of_row == j, e, 0.0) for j in range(HALO)], axis=1).astype(BF16)

    y = _dot(block_diag(er_ref), tr_ref[...]) + _dot(block_diag(ei_ref), ti_ref[...])
    y_ref[...] = y.reshape(y_ref.shape)


def _dft_stage2(tr, ti, er, ei):
    b, n, ch = tr.shape
    n_blk, rows, n2 = er.shape
    k2 = rows // HALO
    t_spec = pl.BlockSpec((None, rows, ch), lambda bi, j: (bi, j, 0))
    g_spec = pl.BlockSpec((None, rows, n2), lambda bi, j: (j, 0, 0))
    return pl.pallas_call(
        _dft2_kernel,
        grid=(b, n_blk),
        in_specs=[t_spec, t_spec, g_spec, g_spec],
        out_specs=pl.BlockSpec((None, k2, HALO, ch), lambda bi, j: (bi, 0, j, 0)),
        out_shape=jax.ShapeDtypeStruct((b, k2, n // k2, ch), F32),
        compiler_params=_params(("arbitrary", "arbitrary")),
        name="dft_stage2",
    )(tr, ti, er, ei)


def _rope_tables(n):
    n_rows = n // GRID_W
    quarter = HEAD_DIM // 4
    freqs = ROPE_THETA ** (-jnp.arange(quarter, dtype=F32) / quarter)
    ar = jnp.arange(n_rows, dtype=F32)[:, None] * freqs
    ac = jnp.arange(GRID_W, dtype=F32)[:, None] * freqs

    def table(fr, fc):
        fr = jnp.broadcast_to(fr[:, None, :], (n_rows, GRID_W, 2 * quarter))
        fc = jnp.broadcast_to(fc[None, :, :], (n_rows, GRID_W, 2 * quarter))
        return jnp.concatenate([fr, fc], axis=-1).reshape(n, HEAD_DIM)

    cr, sr, cc, sc = jnp.cos(ar), jnp.sin(ar), jnp.cos(ac), jnp.sin(ac)
    cos_t = table(jnp.concatenate([cr, cr], axis=-1), jnp.concatenate([cc, cc], axis=-1))
    sin_t = table(jnp.concatenate([-sr, sr], axis=-1), jnp.concatenate([-sc, sc], axis=-1))
    return cos_t, sin_t


def _unit_angle(num, den):
    return (2.0 * math.pi / den) * (num % den).astype(F32)


def _dft_tables(n):
    n1 = 128
    n2 = n // n1
    ch = jnp.arange(FFT_GROUP, dtype=jnp.int32)
    a = _unit_angle(ch[:, None] * ch[None, :], FFT_GROUP)
    cs = jnp.concatenate([jnp.cos(a), -jnp.sin(a)], axis=1).astype(BF16)
    k1 = jnp.arange(n1, dtype=jnp.int32)
    a1 = _unit_angle(k1[:, None] * k1[None, :], n1)
    fr, fi = jnp.cos(a1), -jnp.sin(a1)
    w1 = jnp.concatenate([jnp.concatenate([fr, -fi], axis=1),
                          jnp.concatenate([fi, fr], axis=1)], axis=0).astype(BF16)
    m = jnp.arange(n2, dtype=jnp.int32)
    aa = _unit_angle(k1[:, None] * m[None, :], n // n1)
    ab = _unit_angle(k1[:, None] * m[None, :], n)
    ca, sa = jnp.cos(aa)[None, :, None, :], jnp.sin(aa)[None, :, None, :]
    n_blk = n1 // HALO
    cb = jnp.cos(ab).reshape(n_blk, 1, HALO, n2)
    sb = jnp.sin(ab).reshape(n_blk, 1, HALO, n2)
    norm = 1.0 / math.sqrt(n * FFT_GROUP)
    cos2 = ((ca * cb - sa * sb) * norm).reshape(n_blk, (n // n1) * HALO, n2)
    sin2 = ((sa * cb + ca * sb) * norm).reshape(n_blk, (n // n1) * HALO, n2)
    return cs, w1, cos2, sin2


def kernel(x, c, ctx, c_ctx, norm_g, w_mod, b_mod, w_in_even, w_out_even, q_norm_g, k_norm_g, w_pool,
           pool_scale, w_in_odd, w_out_odd, conv_w, w_ffn_gate, w_ffn_up, w_ffn_down, final_g):
    b, n, d = x.shape
    tm = TILES.proj_rows

    cvec = jnp.zeros((8, d), F32).at[:b].set(c).at[b].set(c_ctx)
    mod = _modulation(cvec, w_mod, b_mod).reshape(w_mod.shape[0], 8, 6, 1, d)

    def mvec(layer, comp, rows=slice(0, b)):
        return mod[layer, rows, comp]

    zeros_vec = jnp.zeros((b, 1, d), F32)

    w0 = w_in_even[0]
    w_qku = w0.astype(BF16)
    w_vt = w0[:, ATTN_W + KV_W:ATTN_W + 2 * KV_W].T.astype(BF16)
    qg = q_norm_g[0].reshape(1, HEAD_DIM)
    kg = k_norm_g[0].reshape(1, HEAD_DIM)
    ng00 = norm_g[0, 0].reshape(1, d)
    cos_t, sin_t = _rope_tables(n)
    q, k, vt, u = _in_proj_even(x, mvec(0, 0), mvec(0, 1), ng00, w_qku, w_vt, qg, kg, cos_t, sin_t, tm)

    lc = ctx.shape[1]
    ctx_rows = slice(b, b + 1)
    ctx_shift = jnp.broadcast_to(mvec(0, 0, ctx_rows), (b, 1, d))
    ctx_scale = jnp.broadcast_to(mvec(0, 1, ctx_rows), (b, 1, d))
    ident_cos = jnp.ones((lc, HEAD_DIM), F32)
    ident_sin = jnp.zeros((lc, HEAD_DIM), F32)
    _, kc, vct, _ = _in_proj_even(ctx, ctx_shift, ctx_scale, ng00, w_qku, w_vt, qg, kg,
                                  ident_cos, ident_sin, lc)

    logit_bound = (math.sqrt(HEAD_DIM) * math.log2(math.e) * 1.02
                   * jnp.max(jnp.abs(q_norm_g[0])) * jnp.max(jnp.abs(k_norm_g[0])))
    attn = functools.partial(_attention, q, k, vt, kc, vct, tq=TILES.attn_queries, tk=TILES.attn_keys)
    o = lax.cond(logit_bound <= SAFE_LOGIT_BOUND,
                 lambda: attn(stabilize=False), lambda: attn(stabilize=True))

    x1, h1 = _out_proj_even(o, u, w_pool[0].astype(BF16), pool_scale[0].reshape(1, POOL_W),
                            w_out_even[0].astype(BF16), x, mvec(0, 2), norm_g[0, 1].reshape(1, d),
                            mvec(0, 3), mvec(0, 4), tm)
    x2, h2 = _ffn(h1, w_ffn_gate, w_ffn_up, 0, w_ffn_down[0].astype(BF16),
                  x1, mvec(0, 5), norm_g[1, 0].reshape(1, d), mvec(1, 0), mvec(1, 1),
                  emit_x=True)

    cs, w1, gr, gi = _dft_tables(n)
    zr, zi, cu, bg = _in_proj_odd(h2, w_in_odd[0].astype(BF16), cs, tm)
    n1 = 128
    tr, ti = _dft_stage1(zr.reshape(b, n1, n // n1, FFT_W), zi.reshape(b, n1, n // n1, FFT_W), w1,
                         TILES.dft_n2, TILES.dft_ch)
    y = _dft_stage2(tr.reshape(b, n, FFT_W), ti.reshape(b, n, FFT_W), gr, gi).reshape(b, n, FFT_W)
    x3, h3 = _out_proj_odd(y, cu, bg, conv_w[0], w_out_odd[0].astype(BF16), x2, mvec(1, 2),
                           norm_g[1, 1].reshape(1, d), mvec(1, 3), mvec(1, 4), tm)
    (out,) = _ffn(h3, w_ffn_gate, w_ffn_up, 1, w_ffn_down[1].astype(BF16),
                  x3, mvec(1, 5), final_g.reshape(1, d), zeros_vec, zeros_vec,
                  emit_x=False)
    return out
```

```python
import functools
import math
from typing import NamedTuple

import jax
import jax.numpy as jnp
from jax import lax
from jax.experimental import pallas as pl
from jax.experimental.pallas import tpu as pltpu

F32 = jnp.float32
BF16 = jnp.bfloat16

D_MODEL = 2048
CTX_LEN = 256
GRID_W = 64
EPS = 1e-6
HEAD_DIM = 128
N_Q_HEADS = 12
N_KV_HEADS = 4
Q_GROUP = N_Q_HEADS // N_KV_HEADS
ATTN_W = N_Q_HEADS * HEAD_DIM
KV_W = N_KV_HEADS * HEAD_DIM
ROPE_THETA = 10000.0
POOL_WINDOWS = (2, 4, 8, 16)
POOL_W = 512
POOL_GROUP = 128
FFT_W = 1024
N_FFT_GROUPS = 4
FFT_GROUP = 256
CONV_W = 1024
HALO = 8
V7X_VMEM_BYTES = 64 * 1024 * 1024
VMEM_LIMIT = V7X_VMEM_BYTES - 8 * 1024 * 1024

QK_SCALE_LOG2 = HEAD_DIM ** -0.5 * math.log2(math.e)
SAFE_LOGIT_BOUND = 80.0

NT_DIMS = (((1,), (1,)), ((), ()))


class Tiles(NamedTuple):
    proj_rows: int = 512
    attn_queries: int = 1024
    attn_keys: int = 1024
    ffn_up_rows: int = 1024
    ffn_up_cols: int = 512
    ffn_down_rows: int = 256
    mod_cols: int = 1024
    dft_n2: int = 16
    dft_ch: int = 512


TILES = Tiles()


def _params(semantics):
    return pltpu.CompilerParams(dimension_semantics=semantics, vmem_limit_bytes=VMEM_LIMIT)


def _dot(a, b):
    return jnp.dot(a, b, preferred_element_type=F32)


def _rms(x, g):
    ms = jnp.mean(x * x, axis=-1, keepdims=True)
    return x * lax.rsqrt(ms + EPS) * g


def _silu(x):
    return x / (1.0 + jnp.exp(-x))


def _resident(shape):
    zeros = (0,) * len(shape)
    return pl.BlockSpec(shape, lambda *_: zeros, pipeline_mode=pl.Buffered(1))


def _mod_kernel(c_ref, w_ref, b_ref, o_ref):
    s = _silu(c_ref[...]).astype(BF16)
    o_ref[...] = _dot(s, w_ref[...].astype(BF16)) + b_ref[...]


def _modulation(cvec, w_mod, b_mod):
    depth, d, n_out = w_mod.shape
    tn = TILES.mod_cols
    return pl.pallas_call(
        _mod_kernel,
        grid=(depth, n_out // tn),
        in_specs=[
            pl.BlockSpec((8, d), lambda l, j: (0, 0)),
            pl.BlockSpec((None, d, tn), lambda l, j: (l, 0, j)),
            pl.BlockSpec((None, 1, tn), lambda l, j: (l, 0, j)),
        ],
        out_specs=pl.BlockSpec((None, 8, tn), lambda l, j: (l, 0, j)),
        out_shape=jax.ShapeDtypeStruct((depth, 8, n_out), F32),
        compiler_params=_params(("arbitrary", "arbitrary")),
        name="adaln_modulation",
    )(cvec, w_mod, b_mod.reshape(depth, 1, n_out))


def _in_even_kernel(x_ref, sh_ref, sc_ref, ng_ref, w_ref, wvt_ref, qg_ref, kg_ref, cos_ref, sin_ref,
                    q_ref, k_ref, vt_ref, u_ref):
    h = _rms(x_ref[...], ng_ref[...]) * (1.0 + sc_ref[...]) + sh_ref[...]
    hb = h.astype(BF16)
    cosf = cos_ref[...]
    sinf = sin_ref[...]
    lane = lax.broadcasted_iota(jnp.int32, cosf.shape, 1)
    first_of_pair = (lane & 32) == 0

    def norm_rope(z, g):
        zn = _rms(z, g)
        partner = jnp.where(first_of_pair, pltpu.roll(zn, 96, 1), pltpu.roll(zn, 32, 1))
        return zn * cosf + partner * sinf

    qg = qg_ref[...]
    kg = kg_ref[...]
    heads_per_chunk = 4
    cw = heads_per_chunk * HEAD_DIM
    for c in range(ATTN_W // cw):
        z = _dot(hb, w_ref[:, c * cw:(c + 1) * cw])
        for j in range(heads_per_chunk):
            col = c * cw + j * HEAD_DIM
            zh = norm_rope(z[:, j * HEAD_DIM:(j + 1) * HEAD_DIM], qg) * QK_SCALE_LOG2
            q_ref[:, col:col + HEAD_DIM] = zh.astype(BF16)
    z = _dot(hb, w_ref[:, ATTN_W:ATTN_W + KV_W])
    for j in range(N_KV_HEADS):
        zh = norm_rope(z[:, j * HEAD_DIM:(j + 1) * HEAD_DIM], kg)
        k_ref[:, j * HEAD_DIM:(j + 1) * HEAD_DIM] = zh.astype(BF16)
    u_ref[...] = _dot(hb, w_ref[:, ATTN_W + 2 * KV_W:])
    vt_ref[...] = lax.dot_general(wvt_ref[...], hb, NT_DIMS, preferred_element_type=F32).astype(BF16)


def _in_proj_even(x, shift, scale, ng, w_qku, w_vt, qg, kg, cos_t, sin_t, tm):
    b, n, d = x.shape
    row = lambda bi, i: (bi, i, 0)
    vec = lambda bi, i: (bi, 0, 0)
    return pl.pallas_call(
        _in_even_kernel,
        grid=(b, n // tm),
        in_specs=[
            pl.BlockSpec((None, tm, d), row),
            pl.BlockSpec((None, 1, d), vec),
            pl.BlockSpec((None, 1, d), vec),
            _resident((1, d)),
            _resident(w_qku.shape),
            _resident(w_vt.shape),
            _resident((1, HEAD_DIM)),
            _resident((1, HEAD_DIM)),
            pl.BlockSpec((tm, HEAD_DIM), lambda bi, i: (i, 0)),
            pl.BlockSpec((tm, HEAD_DIM), lambda bi, i: (i, 0)),
        ],
        out_specs=[
            pl.BlockSpec((None, tm, ATTN_W), row),
            pl.BlockSpec((None, tm, KV_W), row),
            pl.BlockSpec((None, KV_W, tm), lambda bi, i: (bi, 0, i)),
            pl.BlockSpec((None, tm, POOL_W), row),
        ],
        out_shape=[
            jax.ShapeDtypeStruct((b, n, ATTN_W), BF16),
            jax.ShapeDtypeStruct((b, n, KV_W), BF16),
            jax.ShapeDtypeStruct((b, KV_W, n), BF16),
            jax.ShapeDtypeStruct((b, n, POOL_W), F32),
        ],
        compiler_params=_params(("arbitrary", "arbitrary")),
        name="in_proj_even",
    )(x, shift, scale, ng, w_qku, w_vt, qg, kg, cos_t, sin_t)


def _stack_group_queries(q_ref):
    q3 = q_ref[...]
    return jnp.concatenate([q3[:, g * HEAD_DIM:(g + 1) * HEAD_DIM] for g in range(Q_GROUP)], axis=0)


def _write_attention_out(o_ref, acc_ref, l_ref):
    tq = o_ref.shape[0]
    o = acc_ref[...] / l_ref[...]
    for g in range(Q_GROUP):
        o_ref[:, g * HEAD_DIM:(g + 1) * HEAD_DIM] = o[:, g * tq:(g + 1) * tq].T.astype(BF16)


def _flash_pipelined_kernel(q_ref, k_ref, vt_ref, kc_ref, vct_ref, o_ref, acc_ref, l_ref, pa_ref, pb_ref,
                            *, tk):
    n_tiles = k_ref.shape[0] // tk
    qcat = _stack_group_queries(q_ref)

    def probs(kt):
        return jnp.exp2(lax.dot_general(kt, qcat, NT_DIMS, preferred_element_type=F32))

    def offset(tile):
        return tile * tk if isinstance(tile, int) else pl.multiple_of(tile * tk, tk)

    def logits_to(dst_ref, tile):
        p = probs(k_ref[pl.ds(offset(tile), tk), :])
        l_ref[...] += jnp.sum(p, axis=0, keepdims=True)
        dst_ref[...] = p.astype(BF16)

    def values_from(src_ref, tile):
        acc_ref[...] += _dot(vt_ref[:, pl.ds(offset(tile), tk)], src_ref[...])

    pc = probs(kc_ref[...])
    l_ref[...] = jnp.sum(pc, axis=0, keepdims=True)
    acc_ref[...] = _dot(vct_ref[...], pc.astype(BF16))
    logits_to(pa_ref, 0)

    def body(i, carry):
        logits_to(pb_ref, 2 * i + 1)
        values_from(pa_ref, 2 * i)
        logits_to(pa_ref, 2 * i + 2)
        values_from(pb_ref, 2 * i + 1)
        return carry

    lax.fori_loop(0, n_tiles // 2 - 1, body, 0)
    logits_to(pb_ref, n_tiles - 1)
    values_from(pa_ref, n_tiles - 2)
    values_from(pb_ref, n_tiles - 1)
    _write_attention_out(o_ref, acc_ref, l_ref)


def _flash_kernel(q_ref, k_ref, vt_ref, kc_ref, vct_ref, o_ref, acc_ref, l_ref, m_ref, *, tk, stabilize):
    n_keys = k_ref.shape[0]
    qcat = _stack_group_queries(q_ref)

    def step(kt, vtt, first):
        s = lax.dot_general(kt, qcat, NT_DIMS, preferred_element_type=F32)
        if stabilize:
            mx = jnp.max(s, axis=0, keepdims=True)
            m_new = mx if first else jnp.maximum(m_ref[...], mx)
            p = jnp.exp2(s - m_new)
        else:
            p = jnp.exp2(s)
        ps = jnp.sum(p, axis=0, keepdims=True)
        pv = _dot(vtt, p.astype(BF16))
        if first:
            acc_ref[...] = pv
            l_ref[...] = ps
        elif stabilize:
            alpha = jnp.exp2(m_ref[...] - m_new)
            acc_ref[...] = acc_ref[...] * alpha + pv
            l_ref[...] = l_ref[...] * alpha + ps
        else:
            acc_ref[...] += pv
            l_ref[...] += ps
        if stabilize:
            m_ref[...] = m_new

    step(kc_ref[...], vct_ref[...], True)

    def body(j, carry):
        off = pl.multiple_of(j * tk, tk)
        step(k_ref[pl.ds(off, tk), :], vt_ref[:, pl.ds(off, tk)], False)
        return carry

    lax.fori_loop(0, n_keys // tk, body, 0)
    _write_attention_out(o_ref, acc_ref, l_ref)


def _attention(q, k, vt, kc, vct, *, tq, tk, stabilize):
    b, n, _ = q.shape
    lc = kc.shape[1]
    gw = Q_GROUP * HEAD_DIM
    r = Q_GROUP * tq
    if stabilize:
        body = functools.partial(_flash_kernel, tk=tk, stabilize=True)
        scratch = [pltpu.VMEM((1, r), F32)]
    else:
        body = functools.partial(_flash_pipelined_kernel, tk=tk)
        scratch = [pltpu.VMEM((tk, r), BF16), pltpu.VMEM((tk, r), BF16)]
    return pl.pallas_call(
        body,
        grid=(b, N_KV_HEADS, n // tq),
        in_specs=[
            pl.BlockSpec((None, tq, gw), lambda bi, h, i: (bi, i, h)),
            pl.BlockSpec((None, n, HEAD_DIM), lambda bi, h, i: (bi, 0, h), pipeline_mode=pl.Buffered(1)),
            pl.BlockSpec((None, HEAD_DIM, n), lambda bi, h, i: (bi, h, 0), pipeline_mode=pl.Buffered(1)),
            pl.BlockSpec((None, lc, HEAD_DIM), lambda bi, h, i: (bi, 0, h)),
            pl.BlockSpec((None, HEAD_DIM, lc), lambda bi, h, i: (bi, h, 0)),
        ],
        out_specs=pl.BlockSpec((None, tq, gw), lambda bi, h, i: (bi, i, h)),
        out_shape=jax.ShapeDtypeStruct((b, n, ATTN_W), BF16),
        scratch_shapes=[pltpu.VMEM((HEAD_DIM, r), F32), pltpu.VMEM((1, r), F32)] + scratch,
        compiler_params=_params(("arbitrary", "arbitrary", "arbitrary")),
        name="attention_stable" if stabilize else "attention",
    )(q, k, vt, kc, vct)


def _fill_halo(ext_ref, prev_ref, cur_ref, next_ref, i, n_tiles):
    tm = cur_ref.shape[0]
    ext_ref[0:HALO, :] = jnp.where(i > 0, prev_ref[...], 0.0)
    ext_ref[HALO:HALO + tm, :] = cur_ref[...]
    ext_ref[HALO + tm:, :] = jnp.where(i < n_tiles - 1, next_ref[...], 0.0)


def _resid_norm(x, delta, gate, ng, shift, scale):
    x_new = x + gate * delta
    return x_new, _rms(x_new, ng) * (1.0 + scale) + shift


def _out_even_kernel(o_ref, uprev_ref, u_ref, unext_ref, wpool_ref, pscale_ref, wout_ref,
                     x_ref, gate_ref, ng_ref, sh_ref, sc_ref,
                     xo_ref, ho_ref, ext_ref, *, seq_len):
    i = pl.program_id(1)
    tm = u_ref.shape[0]
    _fill_halo(ext_ref, uprev_ref, u_ref, unext_ref, i, pl.num_programs(1))
    t = i * tm + lax.broadcasted_iota(jnp.int32, (tm, POOL_GROUP), 0)
    mix = [o_ref[...]]
    for g, w in enumerate(POOL_WINDOWS):
        cols = slice(g * POOL_GROUP, (g + 1) * POOL_GROUP)
        win = ext_ref[HALO - w // 2:HALO - w // 2 + tm, cols]
        for dlt in range(-w // 2 + 1, w // 2):
            win = win + ext_ref[HALO + dlt:HALO + dlt + tm, cols]
        lo = jnp.clip(t - w // 2, 0, seq_len)
        hi = jnp.clip(t - w // 2 + w, 0, seq_len)
        pooled = win / (hi - lo).astype(F32) - u_ref[:, cols]
        y = _dot(pooled.astype(BF16), wpool_ref[g]) * pscale_ref[:, cols]
        mix.append(y.astype(BF16))
    delta = _dot(jnp.concatenate(mix, axis=1), wout_ref[...])
    x_new, h_new = _resid_norm(x_ref[...], delta, gate_ref[...], ng_ref[...], sh_ref[...], sc_ref[...])
    xo_ref[...] = x_new
    ho_ref[...] = h_new.astype(BF16)


def _out_odd_kernel(y_ref, cprev_ref, c_ref, cnext_ref, bg_ref, cw_ref, wout_ref,
                    x_ref, gate_ref, ng_ref, sh_ref, sc_ref,
                    xo_ref, ho_ref, ext_ref):
    i = pl.program_id(1)
    tm = c_ref.shape[0]
    _fill_halo(ext_ref, cprev_ref, c_ref, cnext_ref, i, pl.num_programs(1))
    halves = [slice(r0, r0 + tm // 2) for r0 in (0, tm // 2)]
    mixes = []
    for rows in halves:
        base = HALO + rows.start
        sub = tm // 2
        conv = (ext_ref[base - 1:base - 1 + sub, :] * cw_ref[0:1, :]
                + ext_ref[base:base + sub, :] * cw_ref[1:2, :]
                + ext_ref[base + 1:base + 1 + sub, :] * cw_ref[2:3, :])
        gated = (bg_ref[rows, :] * conv).astype(BF16)
        mixes.append(jnp.concatenate([y_ref[rows, :].astype(BF16), gated], axis=1))
    deltas = [_dot(mix, wout_ref[...]) for mix in mixes]
    for rows, delta in zip(halves, deltas):
        x_new, h_new = _resid_norm(x_ref[rows, :], delta, gate_ref[...], ng_ref[...], sh_ref[...], sc_ref[...])
        xo_ref[rows, :] = x_new
        ho_ref[rows, :] = h_new.astype(BF16)


def _halo_specs(width, tm, n):
    tiles8 = tm // HALO
    last8 = n // HALO - 1
    return [
        pl.BlockSpec((None, HALO, width), lambda bi, i: (bi, jnp.maximum(i * tiles8 - 1, 0), 0)),
        pl.BlockSpec((None, tm, width), lambda bi, i: (bi, i, 0)),
        pl.BlockSpec((None, HALO, width), lambda bi, i: (bi, jnp.minimum((i + 1) * tiles8, last8), 0)),
    ]


def _resid_specs(tm, d):
    row = pl.BlockSpec((None, tm, d), lambda bi, i: (bi, i, 0))
    vec = pl.BlockSpec((None, 1, d), lambda bi, i: (bi, 0, 0))
    return [row, vec, _resident((1, d)), vec, vec]


def _resid_outs(b, n, d, tm):
    row = lambda bi, i: (bi, i, 0)
    specs = [pl.BlockSpec((None, tm, d), row), pl.BlockSpec((None, tm, d), row)]
    shapes = [jax.ShapeDtypeStruct((b, n, d), F32), jax.ShapeDtypeStruct((b, n, d), BF16)]
    return specs, shapes


def _out_proj_even(o, u, w_pool, pool_scale, w_out, x, gate, ng, shift, scale, tm):
    b, n, d = x.shape
    out_specs, out_shape = _resid_outs(b, n, d, tm)
    return pl.pallas_call(
        functools.partial(_out_even_kernel, seq_len=n),
        grid=(b, n // tm),
        in_specs=[pl.BlockSpec((None, tm, ATTN_W), lambda bi, i: (bi, i, 0))]
        + _halo_specs(POOL_W, tm, n)
        + [_resident(w_pool.shape), _resident((1, POOL_W)), _resident(w_out.shape)]
        + _resid_specs(tm, d),
        out_specs=out_specs,
        out_shape=out_shape,
        scratch_shapes=[pltpu.VMEM((tm + 2 * HALO, POOL_W), F32)],
        compiler_params=_params(("arbitrary", "arbitrary")),
        name="out_proj_even",
    )(o, u, u, u, w_pool, pool_scale, w_out, x, gate, ng, shift, scale)


def _out_proj_odd(y, cu, bg, conv_w, w_out, x, gate, ng, shift, scale, tm):
    b, n, d = x.shape
    out_specs, out_shape = _resid_outs(b, n, d, tm)
    row = lambda bi, i: (bi, i, 0)
    return pl.pallas_call(
        _out_odd_kernel,
        grid=(b, n // tm),
        in_specs=[pl.BlockSpec((None, tm, FFT_W), row)]
        + _halo_specs(CONV_W, tm, n)
        + [pl.BlockSpec((None, tm, CONV_W), row), _resident(conv_w.shape), _resident(w_out.shape)]
        + _resid_specs(tm, d),
        out_specs=out_specs,
        out_shape=out_shape,
        scratch_shapes=[pltpu.VMEM((tm + 2 * HALO, CONV_W), F32)],
        compiler_params=_params(("arbitrary", "arbitrary")),
        name="out_proj_odd",
    )(y, cu, cu, cu, bg, conv_w, w_out, x, gate, ng, shift, scale)


def _ffn_up_kernel(h_ref, wg_ref, wu_ref, wd_ref, a_ref, wdb_ref, wgb_ref, wub_ref):
    @pl.when(pl.program_id(1) == 0)
    def _():
        wgb_ref[...] = wg_ref[...].astype(BF16)
        wub_ref[...] = wu_ref[...].astype(BF16)
        wdb_ref[...] = wd_ref[...].astype(BF16)

    h = h_ref[...]
    a_ref[...] = (_silu(_dot(h, wgb_ref[...])) * _dot(h, wub_ref[...])).astype(BF16)


def _ffn_up(h, wg, wu, wd, layer, *, tm, tn):
    rows, d = h.shape
    d_ff = wg.shape[2]
    w_spec = pl.BlockSpec((None, d, tn), lambda j, i: (layer, 0, j))
    return pl.pallas_call(
        _ffn_up_kernel,
        grid=(d_ff // tn, rows // tm),
        in_specs=[pl.BlockSpec((tm, d), lambda j, i: (i, 0)), w_spec, w_spec,
                  pl.BlockSpec((None, tn, d), lambda j, i: (layer, j, 0))],
        out_specs=[pl.BlockSpec((tm, tn), lambda j, i: (i, j)), pl.BlockSpec((tn, d), lambda j, i: (j, 0))],
        out_shape=[jax.ShapeDtypeStruct((rows, d_ff), BF16), jax.ShapeDtypeStruct((d_ff, d), BF16)],
        scratch_shapes=[pltpu.VMEM((d, tn), BF16), pltpu.VMEM((d, tn), BF16)],
        compiler_params=_params(("arbitrary", "arbitrary")),
        name="ffn_up",
    )(h, wg, wu, wd)


def _ffn_down_kernel(a_ref, wd_ref, x_ref, gate_ref, ng_ref, sh_ref, sc_ref, *out_refs):
    x_new, h_new = _resid_norm(x_ref[...], _dot(a_ref[...], wd_ref[...]), gate_ref[...], ng_ref[...],
                               sh_ref[...], sc_ref[...])
    if len(out_refs) == 2:
        out_refs[0][...] = x_new
    out_refs[-1][...] = h_new.astype(out_refs[-1].dtype)


def _ffn_down(a, wd, x, gate, ng, shift, scale, *, tm, emit_x):
    b, n, d = x.shape
    d_ff = wd.shape[0]
    row = lambda bi, i: (bi, i, 0)
    out_specs = [pl.BlockSpec((None, tm, d), row)]
    out_shape = [jax.ShapeDtypeStruct((b, n, d), BF16 if emit_x else F32)]
    if emit_x:
        out_specs = [pl.BlockSpec((None, tm, d), row)] + out_specs
        out_shape = [jax.ShapeDtypeStruct((b, n, d), F32)] + out_shape
    return pl.pallas_call(
        _ffn_down_kernel,
        grid=(b, n // tm),
        in_specs=[pl.BlockSpec((None, tm, d_ff), row), _resident(wd.shape)] + _resid_specs(tm, d),
        out_specs=out_specs,
        out_shape=out_shape,
        compiler_params=_params(("arbitrary", "arbitrary")),
        name="ffn_down" if emit_x else "ffn_down_final",
    )(a, wd, x, gate, ng, shift, scale)


def _ffn(h, wg, wu, layer, wd, x, gate, ng, shift, scale, *, emit_x):
    b, n, d = x.shape
    a, wd_bf16 = _ffn_up(h.reshape(b * n, d), wg, wu, wd, layer, tm=TILES.ffn_up_rows, tn=TILES.ffn_up_cols)
    return _ffn_down(a.reshape(b, n, -1), wd_bf16, x, gate, ng, shift, scale, tm=TILES.ffn_down_rows,
                     emit_x=emit_x)


def _in_odd_kernel(h_ref, w_ref, cs_ref, zr_ref, zi_ref, cu_ref, bg_ref):
    h = h_ref[...]
    cw = 512
    for c in range(FFT_W // cw):
        f = _dot(h, w_ref[:, c * cw:(c + 1) * cw]).astype(BF16)
        for g in range(cw // FFT_GROUP):
            cols = slice(c * cw + g * FFT_GROUP, c * cw + (g + 1) * FFT_GROUP)
            z = _dot(f[:, g * FFT_GROUP:(g + 1) * FFT_GROUP], cs_ref[...])
            zr_ref[:, cols] = z[:, :FFT_GROUP].astype(BF16)
            zi_ref[:, cols] = z[:, FFT_GROUP:].astype(BF16)
    for c in range(CONV_W // cw):
        lo = c * cw
        bg_ref[:, lo:lo + cw] = _dot(h, w_ref[:, FFT_W + lo:FFT_W + lo + cw])
        cg = _dot(h, w_ref[:, FFT_W + CONV_W + lo:FFT_W + CONV_W + lo + cw])
        xin = _dot(h, w_ref[:, FFT_W + 2 * CONV_W + lo:FFT_W + 2 * CONV_W + lo + cw])
        cu_ref[:, lo:lo + cw] = cg * xin


def _in_proj_odd(h, w_in, cs, tm):
    b, n, d = h.shape
    row = lambda bi, i: (bi, i, 0)
    return pl.pallas_call(
        _in_odd_kernel,
        grid=(b, n // tm),
        in_specs=[pl.BlockSpec((None, tm, d), row), _resident(w_in.shape), _resident(cs.shape)],
        out_specs=[pl.BlockSpec((None, tm, FFT_W), row)] * 2 + [pl.BlockSpec((None, tm, CONV_W), row)] * 2,
        out_shape=[jax.ShapeDtypeStruct((b, n, FFT_W), BF16)] * 2
        + [jax.ShapeDtypeStruct((b, n, CONV_W), F32)] * 2,
        compiler_params=_params(("arbitrary", "arbitrary")),
        name="in_proj_odd",
    )(h, w_in, cs)


def _dft1_kernel(zr_ref, zi_ref, w_ref, tr_ref, ti_ref):
    zr = pltpu.einshape("mjc->jmc", zr_ref[...])
    zi = pltpu.einshape("mjc->jmc", zi_ref[...])
    half = zr_ref.shape[0]
    trs, tis = [], []
    for j in range(zr.shape[0]):
        t = _dot(w_ref[...], jnp.concatenate([zr[j], zi[j]], axis=0))
        trs.append(t[:half].astype(BF16))
        tis.append(t[half:].astype(BF16))
    tr_ref[...] = pltpu.einshape("jmc->mjc", jnp.stack(trs, axis=0))
    ti_ref[...] = pltpu.einshape("jmc->mjc", jnp.stack(tis, axis=0))


def _dft_stage1(zr, zi, w1, jb, ct):
    b, n1, n2, ch = zr.shape
    blk = pl.BlockSpec((None, n1, jb, ct), lambda bi, j, c: (bi, 0, j, c))
    return pl.pallas_call(
        _dft1_kernel,
        grid=(b, n2 // jb, ch // ct),
        in_specs=[blk, blk, _resident(w1.shape)],
        out_specs=[blk, blk],
        out_shape=[jax.ShapeDtypeStruct(zr.shape, BF16)] * 2,
        compiler_params=_params(("arbitrary", "arbitrary", "arbitrary")),
        name="dft_stage1",
    )(zr, zi, w1)


def _dft2_kernel(tr_ref, ti_ref, e_ref, y_ref):
    n2 = tr_ref.shape[0] // HALO
    ys = []
    for j in range(HALO):
        rows = slice(j * n2, (j + 1) * n2)
        t = jnp.concatenate([tr_ref[rows, :], ti_ref[rows, :]], axis=0)
        ys.append(_dot(e_ref[j], t))
    y_ref[...] = pltpu.einshape("jmc->mjc", jnp.stack(ys, axis=0))


def _dft_stage2(tr, ti, e):
    b, n, ch = tr.shape
    n_blk, _, k2, n2x2 = e.shape
    rows = HALO * n2x2 // 2
    t_spec = pl.BlockSpec((None, rows, ch), lambda bi, j: (bi, j, 0))
    return pl.pallas_call(
        _dft2_kernel,
        grid=(b, n_blk),
        in_specs=[t_spec, t_spec, pl.BlockSpec((None, HALO, k2, n2x2), lambda bi, j: (j, 0, 0, 0))],
        out_specs=pl.BlockSpec((None, k2, HALO, ch), lambda bi, j: (bi, 0, j, 0)),
        out_shape=jax.ShapeDtypeStruct((b, k2, n // k2, ch), F32),
        compiler_params=_params(("arbitrary", "arbitrary")),
        name="dft_stage2",
    )(tr, ti, e)


def _rope_tables(n):
    n_rows = n // GRID_W
    quarter = HEAD_DIM // 4
    freqs = ROPE_THETA ** (-jnp.arange(quarter, dtype=F32) / quarter)
    ar = jnp.arange(n_rows, dtype=F32)[:, None] * freqs
    ac = jnp.arange(GRID_W, dtype=F32)[:, None] * freqs

    def table(fr, fc):
        fr = jnp.broadcast_to(fr[:, None, :], (n_rows, GRID_W, 2 * quarter))
        fc = jnp.broadcast_to(fc[None, :, :], (n_rows, GRID_W, 2 * quarter))
        return jnp.concatenate([fr, fc], axis=-1).reshape(n, HEAD_DIM)

    cr, sr, cc, sc = jnp.cos(ar), jnp.sin(ar), jnp.cos(ac), jnp.sin(ac)
    cos_t = table(jnp.concatenate([cr, cr], axis=-1), jnp.concatenate([cc, cc], axis=-1))
    sin_t = table(jnp.concatenate([-sr, sr], axis=-1), jnp.concatenate([-sc, sc], axis=-1))
    return cos_t, sin_t


def _unit_angle(num, den):
    return (2.0 * math.pi / den) * (num % den).astype(F32)


def _dft_tables(n):
    n1 = 128
    n2 = n // n1
    ch = jnp.arange(FFT_GROUP, dtype=jnp.int32)
    a = _unit_angle(ch[:, None] * ch[None, :], FFT_GROUP)
    cs = jnp.concatenate([jnp.cos(a), -jnp.sin(a)], axis=1).astype(BF16)
    k1 = jnp.arange(n1, dtype=jnp.int32)
    a1 = _unit_angle(k1[:, None] * k1[None, :], n1)
    fr, fi = jnp.cos(a1), -jnp.sin(a1)
    w1 = jnp.concatenate([jnp.concatenate([fr, -fi], axis=1),
                          jnp.concatenate([fi, fr], axis=1)], axis=0).astype(BF16)
    m = jnp.arange(n2, dtype=jnp.int32)
    aa = _unit_angle(k1[:, None] * m[None, :], n // n1)
    ab = _unit_angle(k1[:, None] * m[None, :], n)
    ca, sa = jnp.cos(aa)[None, None, :, :], jnp.sin(aa)[None, None, :, :]
    n_blk = n1 // HALO
    cb = jnp.cos(ab).reshape(n_blk, HALO, 1, n2)
    sb = jnp.sin(ab).reshape(n_blk, HALO, 1, n2)
    norm = 1.0 / math.sqrt(n * FFT_GROUP)
    e2 = jnp.concatenate([(ca * cb - sa * sb) * norm, (sa * cb + ca * sb) * norm], axis=-1).astype(BF16)
    return cs, w1, e2


def kernel(x, c, ctx, c_ctx, norm_g, w_mod, b_mod, w_in_even, w_out_even, q_norm_g, k_norm_g, w_pool,
           pool_scale, w_in_odd, w_out_odd, conv_w, w_ffn_gate, w_ffn_up, w_ffn_down, final_g):
    b, n, d = x.shape
    tm = TILES.proj_rows

    cvec = jnp.zeros((8, d), F32).at[:b].set(c).at[b].set(c_ctx)
    mod = _modulation(cvec, w_mod, b_mod).reshape(w_mod.shape[0], 8, 6, 1, d)

    def mvec(layer, comp, rows=slice(0, b)):
        return mod[layer, rows, comp]

    zeros_vec = jnp.zeros((b, 1, d), F32)

    w0 = w_in_even[0]
    w_qku = w0.astype(BF16)
    w_vt = w_qku[:, ATTN_W + KV_W:ATTN_W + 2 * KV_W].T
    qg = q_norm_g[0].reshape(1, HEAD_DIM)
    kg = k_norm_g[0].reshape(1, HEAD_DIM)
    ng00 = norm_g[0, 0].reshape(1, d)
    cos_t, sin_t = _rope_tables(n)
    q, k, vt, u = _in_proj_even(x, mvec(0, 0), mvec(0, 1), ng00, w_qku, w_vt, qg, kg, cos_t, sin_t, tm)

    lc = ctx.shape[1]
    ctx_rows = slice(b, b + 1)
    ctx_shift = jnp.broadcast_to(mvec(0, 0, ctx_rows), (b, 1, d))
    ctx_scale = jnp.broadcast_to(mvec(0, 1, ctx_rows), (b, 1, d))
    ident_cos = jnp.ones((lc, HEAD_DIM), F32)
    ident_sin = jnp.zeros((lc, HEAD_DIM), F32)
    _, kc, vct, _ = _in_proj_even(ctx, ctx_shift, ctx_scale, ng00, w_qku, w_vt, qg, kg,
                                  ident_cos, ident_sin, lc)

    logit_bound = (math.sqrt(HEAD_DIM) * math.log2(math.e) * 1.02
                   * jnp.max(jnp.abs(q_norm_g[0])) * jnp.max(jnp.abs(k_norm_g[0])))
    attn = functools.partial(_attention, q, k, vt, kc, vct, tq=TILES.attn_queries, tk=TILES.attn_keys)
    o = lax.cond(logit_bound <= SAFE_LOGIT_BOUND,
                 lambda: attn(stabilize=False), lambda: attn(stabilize=True))

    x1, h1 = _out_proj_even(o, u, w_pool[0].astype(BF16), pool_scale[0].reshape(1, POOL_W),
                            w_out_even[0].astype(BF16), x, mvec(0, 2), norm_g[0, 1].reshape(1, d),
                            mvec(0, 3), mvec(0, 4), tm)
    x2, h2 = _ffn(h1, w_ffn_gate, w_ffn_up, 0, w_ffn_down,
                  x1, mvec(0, 5), norm_g[1, 0].reshape(1, d), mvec(1, 0), mvec(1, 1),
                  emit_x=True)

    cs, w1, e2 = _dft_tables(n)
    zr, zi, cu, bg = _in_proj_odd(h2, w_in_odd[0].astype(BF16), cs, tm)
    n1 = 128
    tr, ti = _dft_stage1(zr.reshape(b, n1, n // n1, FFT_W), zi.reshape(b, n1, n // n1, FFT_W), w1,
                         TILES.dft_n2, TILES.dft_ch)
    y = _dft_stage2(tr.reshape(b, n, FFT_W), ti.reshape(b, n, FFT_W), e2).reshape(b, n, FFT_W)
    x3, h3 = _out_proj_odd(y, cu, bg, conv_w[0], w_out_odd[0].astype(BF16), x2, mvec(1, 2),
                           norm_g[1, 1].reshape(1, d), mvec(1, 3), mvec(1, 4), tm)
    (out,) = _ffn(h3, w_ffn_gate, w_ffn_up, 1, w_ffn_down,
                  x3, mvec(1, 5), final_g.reshape(1, d), zeros_vec, zeros_vec,
                  emit_x=False)
    return out
```

```python
import functools
import math
from typing import NamedTuple

import jax
import jax.numpy as jnp
from jax import lax
from jax.experimental import pallas as pl
from jax.experimental.pallas import tpu as pltpu

F32 = jnp.float32
BF16 = jnp.bfloat16

D_MODEL = 2048
CTX_LEN = 256
GRID_W = 64
EPS = 1e-6
HEAD_DIM = 128
N_Q_HEADS = 12
N_KV_HEADS = 4
Q_GROUP = N_Q_HEADS // N_KV_HEADS
ATTN_W = N_Q_HEADS * HEAD_DIM
KV_W = N_KV_HEADS * HEAD_DIM
ROPE_THETA = 10000.0
POOL_WINDOWS = (2, 4, 8, 16)
POOL_W = 512
POOL_GROUP = 128
FFT_W = 1024
N_FFT_GROUPS = 4
FFT_GROUP = 256
CONV_W = 1024
HALO = 8
V7X_VMEM_BYTES = 64 * 1024 * 1024
VMEM_LIMIT = V7X_VMEM_BYTES - 8 * 1024 * 1024

QK_SCALE_LOG2 = HEAD_DIM ** -0.5 * math.log2(math.e)
SAFE_LOGIT_BOUND = 80.0

NT_DIMS = (((1,), (1,)), ((), ()))


class Tiles(NamedTuple):
    proj_rows: int = 512
    attn_queries: int = 1024
    attn_keys: int = 1024
    ffn_up_rows: int = 1024
    ffn_up_cols: int = 512
    ffn_down_rows: int = 512
    ffn_down_sub: int = 256
    mod_cols: int = 1024
    dft_n2: int = 16
    dft_ch: int = 512


TILES = Tiles()


def _params(semantics):
    return pltpu.CompilerParams(dimension_semantics=semantics, vmem_limit_bytes=VMEM_LIMIT)


def _dot(a, b):
    return jnp.dot(a, b, preferred_element_type=F32)


def _rms(x, g):
    ms = jnp.mean(x * x, axis=-1, keepdims=True)
    return x * lax.rsqrt(ms + EPS) * g


def _rms_modulate(x, g, shift, scale):
    return _rms(x, g * (1.0 + scale)) + shift


def _silu(x):
    return x / (1.0 + jnp.exp(-x))


def _resident(shape):
    zeros = (0,) * len(shape)
    return pl.BlockSpec(shape, lambda *_: zeros, pipeline_mode=pl.Buffered(1))


def _mod_kernel(c_ref, w_ref, b_ref, o_ref):
    s = _silu(c_ref[...]).astype(BF16)
    o_ref[...] = _dot(s, w_ref[...].astype(BF16)) + b_ref[...]


def _modulation(cvec, w_mod, b_mod):
    depth, d, n_out = w_mod.shape
    tn = TILES.mod_cols
    return pl.pallas_call(
        _mod_kernel,
        grid=(depth, n_out // tn),
        in_specs=[
            pl.BlockSpec((8, d), lambda l, j: (0, 0)),
            pl.BlockSpec((None, d, tn), lambda l, j: (l, 0, j)),
            pl.BlockSpec((None, 1, tn), lambda l, j: (l, 0, j)),
        ],
        out_specs=pl.BlockSpec((None, 8, tn), lambda l, j: (l, 0, j)),
        out_shape=jax.ShapeDtypeStruct((depth, 8, n_out), F32),
        compiler_params=_params(("arbitrary", "arbitrary")),
        name="adaln_modulation",
    )(cvec, w_mod, b_mod.reshape(depth, 1, n_out))


def _in_even_kernel(x_ref, sh_ref, sc_ref, ng_ref, w_ref, wvt_ref, qg_ref, kg_ref, cos_ref, sin_ref,
                    q_ref, k_ref, vt_ref, u_ref):
    h = _rms_modulate(x_ref[...], ng_ref[...], sh_ref[...], sc_ref[...])
    hb = h.astype(BF16)
    cosf = cos_ref[...]
    sinf = sin_ref[...]
    lane = lax.broadcasted_iota(jnp.int32, cosf.shape, 1)
    first_of_pair = (lane & 32) == 0

    def norm_rope(z, g):
        zn = _rms(z, g)
        partner = jnp.where(first_of_pair, pltpu.roll(zn, 96, 1), pltpu.roll(zn, 32, 1))
        return zn * cosf + partner * sinf

    qg = qg_ref[...]
    kg = kg_ref[...]
    heads_per_chunk = 4
    cw = heads_per_chunk * HEAD_DIM
    for c in range(ATTN_W // cw):
        z = _dot(hb, w_ref[:, c * cw:(c + 1) * cw])
        for j in range(heads_per_chunk):
            col = c * cw + j * HEAD_DIM
            zh = norm_rope(z[:, j * HEAD_DIM:(j + 1) * HEAD_DIM], qg) * QK_SCALE_LOG2
            q_ref[:, col:col + HEAD_DIM] = zh.astype(BF16)
    z = _dot(hb, w_ref[:, ATTN_W:ATTN_W + KV_W])
    for j in range(N_KV_HEADS):
        zh = norm_rope(z[:, j * HEAD_DIM:(j + 1) * HEAD_DIM], kg)
        k_ref[:, j * HEAD_DIM:(j + 1) * HEAD_DIM] = zh.astype(BF16)
    u_ref[...] = _dot(hb, w_ref[:, ATTN_W + 2 * KV_W:])
    vt_ref[...] = lax.dot_general(wvt_ref[...], hb, NT_DIMS, preferred_element_type=F32).astype(BF16)


def _in_proj_even(x, shift, scale, ng, w_qku, w_vt, qg, kg, cos_t, sin_t, tm):
    b, n, d = x.shape
    row = lambda bi, i: (bi, i, 0)
    vec = lambda bi, i: (bi, 0, 0)
    return pl.pallas_call(
        _in_even_kernel,
        grid=(b, n // tm),
        in_specs=[
            pl.BlockSpec((None, tm, d), row),
            pl.BlockSpec((None, 1, d), vec),
            pl.BlockSpec((None, 1, d), vec),
            _resident((1, d)),
            _resident(w_qku.shape),
            _resident(w_vt.shape),
            _resident((1, HEAD_DIM)),
            _resident((1, HEAD_DIM)),
            pl.BlockSpec((tm, HEAD_DIM), lambda bi, i: (i, 0)),
            pl.BlockSpec((tm, HEAD_DIM), lambda bi, i: (i, 0)),
        ],
        out_specs=[
            pl.BlockSpec((None, tm, ATTN_W), row),
            pl.BlockSpec((None, tm, KV_W), row),
            pl.BlockSpec((None, KV_W, tm), lambda bi, i: (bi, 0, i)),
            pl.BlockSpec((None, tm, POOL_W), row),
        ],
        out_shape=[
            jax.ShapeDtypeStruct((b, n, ATTN_W), BF16),
            jax.ShapeDtypeStruct((b, n, KV_W), BF16),
            jax.ShapeDtypeStruct((b, KV_W, n), BF16),
            jax.ShapeDtypeStruct((b, n, POOL_W), F32),
        ],
        compiler_params=_params(("arbitrary", "arbitrary")),
        name="in_proj_even",
    )(x, shift, scale, ng, w_qku, w_vt, qg, kg, cos_t, sin_t)


def _stack_group_queries(q_ref):
    q3 = q_ref[...]
    return jnp.concatenate([q3[:, g * HEAD_DIM:(g + 1) * HEAD_DIM] for g in range(Q_GROUP)], axis=0)


def _write_attention_out(o_ref, acc_ref, l_ref):
    tq = o_ref.shape[0]
    o = acc_ref[...] / l_ref[...]
    for g in range(Q_GROUP):
        o_ref[:, g * HEAD_DIM:(g + 1) * HEAD_DIM] = o[:, g * tq:(g + 1) * tq].T.astype(BF16)


def _flash_pipelined_kernel(q_ref, k_ref, vt_ref, kc_ref, vct_ref, o_ref, acc_ref, l_ref, pa_ref, pb_ref,
                            *, tk):
    n_tiles = k_ref.shape[0] // tk
    qcat = _stack_group_queries(q_ref)

    def probs(kt):
        return jnp.exp2(lax.dot_general(kt, qcat, NT_DIMS, preferred_element_type=F32))

    def offset(tile):
        return tile * tk if isinstance(tile, int) else pl.multiple_of(tile * tk, tk)

    def logits_to(dst_ref, tile):
        p = probs(k_ref[pl.ds(offset(tile), tk), :])
        l_ref[...] += jnp.sum(p, axis=0, keepdims=True)
        dst_ref[...] = p.astype(BF16)

    def values_from(src_ref, tile):
        acc_ref[...] += _dot(vt_ref[:, pl.ds(offset(tile), tk)], src_ref[...])

    pc = probs(kc_ref[...])
    l_ref[...] = jnp.sum(pc, axis=0, keepdims=True)
    acc_ref[...] = _dot(vct_ref[...], pc.astype(BF16))
    logits_to(pa_ref, 0)

    def body(i, carry):
        logits_to(pb_ref, 2 * i + 1)
        values_from(pa_ref, 2 * i)
        logits_to(pa_ref, 2 * i + 2)
        values_from(pb_ref, 2 * i + 1)
        return carry

    lax.fori_loop(0, n_tiles // 2 - 1, body, 0)
    logits_to(pb_ref, n_tiles - 1)
    values_from(pa_ref, n_tiles - 2)
    values_from(pb_ref, n_tiles - 1)
    _write_attention_out(o_ref, acc_ref, l_ref)


def _flash_kernel(q_ref, k_ref, vt_ref, kc_ref, vct_ref, o_ref, acc_ref, l_ref, m_ref, *, tk, stabilize):
    n_keys = k_ref.shape[0]
    qcat = _stack_group_queries(q_ref)

    def step(kt, vtt, first):
        s = lax.dot_general(kt, qcat, NT_DIMS, preferred_element_type=F32)
        if stabilize:
            mx = jnp.max(s, axis=0, keepdims=True)
            m_new = mx if first else jnp.maximum(m_ref[...], mx)
            p = jnp.exp2(s - m_new)
        else:
            p = jnp.exp2(s)
        ps = jnp.sum(p, axis=0, keepdims=True)
        pv = _dot(vtt, p.astype(BF16))
        if first:
            acc_ref[...] = pv
            l_ref[...] = ps
        elif stabilize:
            alpha = jnp.exp2(m_ref[...] - m_new)
            acc_ref[...] = acc_ref[...] * alpha + pv
            l_ref[...] = l_ref[...] * alpha + ps
        else:
            acc_ref[...] += pv
            l_ref[...] += ps
        if stabilize:
            m_ref[...] = m_new

    step(kc_ref[...], vct_ref[...], True)

    def body(j, carry):
        off = pl.multiple_of(j * tk, tk)
        step(k_ref[pl.ds(off, tk), :], vt_ref[:, pl.ds(off, tk)], False)
        return carry

    lax.fori_loop(0, n_keys // tk, body, 0)
    _write_attention_out(o_ref, acc_ref, l_ref)


def _attention(q, k, vt, kc, vct, *, tq, tk, stabilize):
    b, n, _ = q.shape
    lc = kc.shape[1]
    gw = Q_GROUP * HEAD_DIM
    r = Q_GROUP * tq
    if stabilize:
        body = functools.partial(_flash_kernel, tk=tk, stabilize=True)
        scratch = [pltpu.VMEM((1, r), F32)]
    else:
        body = functools.partial(_flash_pipelined_kernel, tk=tk)
        scratch = [pltpu.VMEM((tk, r), BF16), pltpu.VMEM((tk, r), BF16)]
    return pl.pallas_call(
        body,
        grid=(b, N_KV_HEADS, n // tq),
        in_specs=[
            pl.BlockSpec((None, tq, gw), lambda bi, h, i: (bi, i, h)),
            pl.BlockSpec((None, n, HEAD_DIM), lambda bi, h, i: (bi, 0, h)),
            pl.BlockSpec((None, HEAD_DIM, n), lambda bi, h, i: (bi, h, 0)),
            pl.BlockSpec((None, lc, HEAD_DIM), lambda bi, h, i: (bi, 0, h)),
            pl.BlockSpec((None, HEAD_DIM, lc), lambda bi, h, i: (bi, h, 0)),
        ],
        out_specs=pl.BlockSpec((None, tq, gw), lambda bi, h, i: (bi, i, h)),
        out_shape=jax.ShapeDtypeStruct((b, n, ATTN_W), BF16),
        scratch_shapes=[pltpu.VMEM((HEAD_DIM, r), F32), pltpu.VMEM((1, r), F32)] + scratch,
        compiler_params=_params(("arbitrary", "arbitrary", "arbitrary")),
        name="attention_stable" if stabilize else "attention",
    )(q, k, vt, kc, vct)


def _fill_halo(ext_ref, prev_ref, cur_ref, next_ref, i, n_tiles):
    tm = cur_ref.shape[0]
    ext_ref[0:HALO, :] = jnp.where(i > 0, prev_ref[...], 0.0)
    ext_ref[HALO:HALO + tm, :] = cur_ref[...]
    ext_ref[HALO + tm:, :] = jnp.where(i < n_tiles - 1, next_ref[...], 0.0)


def _resid_norm(x, delta, gate, ng, shift, scale):
    x_new = x + gate * delta
    return x_new, _rms_modulate(x_new, ng, shift, scale)


def _out_even_kernel(o_ref, uprev_ref, u_ref, unext_ref, wpool_ref, pscale_ref, wout_ref,
                     x_ref, gate_ref, ng_ref, sh_ref, sc_ref,
                     xo_ref, ho_ref, ext_ref, *, seq_len):
    i = pl.program_id(1)
    tm = u_ref.shape[0]
    _fill_halo(ext_ref, uprev_ref, u_ref, unext_ref, i, pl.num_programs(1))
    t = i * tm + lax.broadcasted_iota(jnp.int32, (tm, POOL_GROUP), 0)
    mix = [o_ref[...]]
    for g, w in enumerate(POOL_WINDOWS):
        cols = slice(g * POOL_GROUP, (g + 1) * POOL_GROUP)
        win = ext_ref[HALO - w // 2:HALO - w // 2 + tm, cols]
        for dlt in range(-w // 2 + 1, w // 2):
            win = win + ext_ref[HALO + dlt:HALO + dlt + tm, cols]
        lo = jnp.clip(t - w // 2, 0, seq_len)
        hi = jnp.clip(t - w // 2 + w, 0, seq_len)
        pooled = win / (hi - lo).astype(F32) - u_ref[:, cols]
        y = _dot(pooled.astype(BF16), wpool_ref[g]) * pscale_ref[:, cols]
        mix.append(y.astype(BF16))
    delta = _dot(jnp.concatenate(mix, axis=1), wout_ref[...])
    x_new, h_new = _resid_norm(x_ref[...], delta, gate_ref[...], ng_ref[...], sh_ref[...], sc_ref[...])
    xo_ref[...] = x_new
    ho_ref[...] = h_new.astype(BF16)


def _out_odd_kernel(y_ref, cprev_ref, c_ref, cnext_ref, bg_ref, cw_ref, wout_ref,
                    x_ref, gate_ref, ng_ref, sh_ref, sc_ref,
                    xo_ref, ho_ref, ext_ref):
    i = pl.program_id(1)
    tm = c_ref.shape[0]
    _fill_halo(ext_ref, cprev_ref, c_ref, cnext_ref, i, pl.num_programs(1))
    halves = [slice(r0, r0 + tm // 2) for r0 in (0, tm // 2)]
    mixes = []
    for rows in halves:
        base = HALO + rows.start
        sub = tm // 2
        conv = (ext_ref[base - 1:base - 1 + sub, :] * cw_ref[0:1, :]
                + ext_ref[base:base + sub, :] * cw_ref[1:2, :]
                + ext_ref[base + 1:base + 1 + sub, :] * cw_ref[2:3, :])
        gated = (bg_ref[rows, :] * conv).astype(BF16)
        mixes.append(jnp.concatenate([y_ref[rows, :].astype(BF16), gated], axis=1))
    deltas = [_dot(mix, wout_ref[...]) for mix in mixes]
    for rows, delta in zip(halves, deltas):
        x_new, h_new = _resid_norm(x_ref[rows, :], delta, gate_ref[...], ng_ref[...], sh_ref[...], sc_ref[...])
        xo_ref[rows, :] = x_new
        ho_ref[rows, :] = h_new.astype(BF16)


def _halo_specs(width, tm, n):
    tiles8 = tm // HALO
    last8 = n // HALO - 1
    return [
        pl.BlockSpec((None, HALO, width), lambda bi, i: (bi, jnp.maximum(i * tiles8 - 1, 0), 0)),
        pl.BlockSpec((None, tm, width), lambda bi, i: (bi, i, 0)),
        pl.BlockSpec((None, HALO, width), lambda bi, i: (bi, jnp.minimum((i + 1) * tiles8, last8), 0)),
    ]


def _resid_specs(tm, d):
    row = pl.BlockSpec((None, tm, d), lambda bi, i: (bi, i, 0))
    vec = pl.BlockSpec((None, 1, d), lambda bi, i: (bi, 0, 0))
    return [row, vec, _resident((1, d)), vec, vec]


def _resid_outs(b, n, d, tm):
    row = lambda bi, i: (bi, i, 0)
    specs = [pl.BlockSpec((None, tm, d), row), pl.BlockSpec((None, tm, d), row)]
    shapes = [jax.ShapeDtypeStruct((b, n, d), F32), jax.ShapeDtypeStruct((b, n, d), BF16)]
    return specs, shapes


def _out_proj_even(o, u, w_pool, pool_scale, w_out, x, gate, ng, shift, scale, tm):
    b, n, d = x.shape
    out_specs, out_shape = _resid_outs(b, n, d, tm)
    return pl.pallas_call(
        functools.partial(_out_even_kernel, seq_len=n),
        grid=(b, n // tm),
        in_specs=[pl.BlockSpec((None, tm, ATTN_W), lambda bi, i: (bi, i, 0))]
        + _halo_specs(POOL_W, tm, n)
        + [_resident(w_pool.shape), _resident((1, POOL_W)), _resident(w_out.shape)]
        + _resid_specs(tm, d),
        out_specs=out_specs,
        out_shape=out_shape,
        scratch_shapes=[pltpu.VMEM((tm + 2 * HALO, POOL_W), F32)],
        compiler_params=_params(("arbitrary", "arbitrary")),
        name="out_proj_even",
    )(o, u, u, u, w_pool, pool_scale, w_out, x, gate, ng, shift, scale)


def _out_proj_odd(y, cu, bg, conv_w, w_out, x, gate, ng, shift, scale, tm):
    b, n, d = x.shape
    out_specs, out_shape = _resid_outs(b, n, d, tm)
    row = lambda bi, i: (bi, i, 0)
    return pl.pallas_call(
        _out_odd_kernel,
        grid=(b, n // tm),
        in_specs=[pl.BlockSpec((None, tm, FFT_W), row)]
        + _halo_specs(CONV_W, tm, n)
        + [pl.BlockSpec((None, tm, CONV_W), row), _resident(conv_w.shape), _resident(w_out.shape)]
        + _resid_specs(tm, d),
        out_specs=out_specs,
        out_shape=out_shape,
        scratch_shapes=[pltpu.VMEM((tm + 2 * HALO, CONV_W), F32)],
        compiler_params=_params(("arbitrary", "arbitrary")),
        name="out_proj_odd",
    )(y, cu, cu, cu, bg, conv_w, w_out, x, gate, ng, shift, scale)


def _ffn_up_kernel(h_ref, wg_ref, wu_ref, a_ref, wgb_ref, wub_ref):
    @pl.when(pl.program_id(1) == 0)
    def _():
        wgb_ref[...] = wg_ref[...].astype(BF16)
        wub_ref[...] = wu_ref[...].astype(BF16)

    h = h_ref[...]
    a_ref[...] = (_silu(_dot(h, wgb_ref[...])) * _dot(h, wub_ref[...])).astype(BF16)


def _ffn_up(h, wg, wu, layer, *, tm, tn):
    rows, d = h.shape
    d_ff = wg.shape[2]
    w_spec = pl.BlockSpec((None, d, tn), lambda j, i: (layer, 0, j))
    return pl.pallas_call(
        _ffn_up_kernel,
        grid=(d_ff // tn, rows // tm),
        in_specs=[pl.BlockSpec((tm, d), lambda j, i: (i, 0)), w_spec, w_spec],
        out_specs=pl.BlockSpec((tm, tn), lambda j, i: (i, j)),
        out_shape=jax.ShapeDtypeStruct((rows, d_ff), BF16),
        scratch_shapes=[pltpu.VMEM((d, tn), BF16), pltpu.VMEM((d, tn), BF16)],
        compiler_params=_params(("arbitrary", "arbitrary")),
        name="ffn_up",
    )(h, wg, wu)


def _ffn_down_kernel(a_ref, wd_ref, x_ref, gate_ref, ng_ref, sh_ref, sc_ref, *out_refs, sub):
    for r0 in range(0, a_ref.shape[0], sub):
        rows = slice(r0, r0 + sub)
        x_new, h_new = _resid_norm(x_ref[rows, :], _dot(a_ref[rows, :], wd_ref[...]), gate_ref[...],
                                   ng_ref[...], sh_ref[...], sc_ref[...])
        if len(out_refs) == 2:
            out_refs[0][rows, :] = x_new
        out_refs[-1][rows, :] = h_new.astype(out_refs[-1].dtype)


def _ffn_down(a, wd, x, gate, ng, shift, scale, *, tm, emit_x):
    b, n, d = x.shape
    d_ff = wd.shape[0]
    row = lambda bi, i: (bi, i, 0)
    out_specs = [pl.BlockSpec((None, tm, d), row)]
    out_shape = [jax.ShapeDtypeStruct((b, n, d), BF16 if emit_x else F32)]
    if emit_x:
        out_specs = [pl.BlockSpec((None, tm, d), row)] + out_specs
        out_shape = [jax.ShapeDtypeStruct((b, n, d), F32)] + out_shape
    return pl.pallas_call(
        functools.partial(_ffn_down_kernel, sub=TILES.ffn_down_sub),
        grid=(b, n // tm),
        in_specs=[pl.BlockSpec((None, tm, d_ff), row), _resident(wd.shape)] + _resid_specs(tm, d),
        out_specs=out_specs,
        out_shape=out_shape,
        compiler_params=_params(("arbitrary", "arbitrary")),
        name="ffn_down" if emit_x else "ffn_down_final",
    )(a, wd, x, gate, ng, shift, scale)


def _ffn(h, wg, wu, layer, wd, x, gate, ng, shift, scale, *, emit_x):
    b, n, d = x.shape
    a = _ffn_up(h.reshape(b * n, d), wg, wu, layer, tm=TILES.ffn_up_rows, tn=TILES.ffn_up_cols)
    return _ffn_down(a.reshape(b, n, -1), wd, x, gate, ng, shift, scale, tm=TILES.ffn_down_rows,
                     emit_x=emit_x)


def _in_odd_kernel(h_ref, w_ref, cs_ref, zr_ref, zi_ref, cu_ref, bg_ref):
    h = h_ref[...]
    cw = 512
    for c in range(FFT_W // cw):
        f = _dot(h, w_ref[:, c * cw:(c + 1) * cw]).astype(BF16)
        for g in range(cw // FFT_GROUP):
            cols = slice(c * cw + g * FFT_GROUP, c * cw + (g + 1) * FFT_GROUP)
            z = _dot(f[:, g * FFT_GROUP:(g + 1) * FFT_GROUP], cs_ref[...])
            zr_ref[:, cols] = z[:, :FFT_GROUP].astype(BF16)
            zi_ref[:, cols] = z[:, FFT_GROUP:].astype(BF16)
    for c in range(CONV_W // cw):
        lo = c * cw
        bg_ref[:, lo:lo + cw] = _dot(h, w_ref[:, FFT_W + lo:FFT_W + lo + cw])
        cg = _dot(h, w_ref[:, FFT_W + CONV_W + lo:FFT_W + CONV_W + lo + cw])
        xin = _dot(h, w_ref[:, FFT_W + 2 * CONV_W + lo:FFT_W + 2 * CONV_W + lo + cw])
        cu_ref[:, lo:lo + cw] = cg * xin


def _in_proj_odd(h, w_in, cs, tm):
    b, n, d = h.shape
    row = lambda bi, i: (bi, i, 0)
    return pl.pallas_call(
        _in_odd_kernel,
        grid=(b, n // tm),
        in_specs=[pl.BlockSpec((None, tm, d), row), _resident(w_in.shape), _resident(cs.shape)],
        out_specs=[pl.BlockSpec((None, tm, FFT_W), row)] * 2 + [pl.BlockSpec((None, tm, CONV_W), row)] * 2,
        out_shape=[jax.ShapeDtypeStruct((b, n, FFT_W), BF16)] * 2
        + [jax.ShapeDtypeStruct((b, n, CONV_W), F32)] * 2,
        compiler_params=_params(("arbitrary", "arbitrary")),
        name="in_proj_odd",
    )(h, w_in, cs)


def _dft1_kernel(zr_ref, zi_ref, w_ref, tr_ref, ti_ref):
    zr = pltpu.einshape("mjc->jmc", zr_ref[...])
    zi = pltpu.einshape("mjc->jmc", zi_ref[...])
    half = zr_ref.shape[0]
    trs, tis = [], []
    for j in range(zr.shape[0]):
        t = _dot(w_ref[...], jnp.concatenate([zr[j], zi[j]], axis=0))
        trs.append(t[:half].astype(BF16))
        tis.append(t[half:].astype(BF16))
    tr_ref[...] = pltpu.einshape("jmc->mjc", jnp.stack(trs, axis=0))
    ti_ref[...] = pltpu.einshape("jmc->mjc", jnp.stack(tis, axis=0))


def _dft_stage1(zr, zi, w1, jb, ct):
    b, n1, n2, ch = zr.shape
    blk = pl.BlockSpec((None, n1, jb, ct), lambda bi, j, c: (bi, 0, j, c))
    return pl.pallas_call(
        _dft1_kernel,
        grid=(b, n2 // jb, ch // ct),
        in_specs=[blk, blk, _resident(w1.shape)],
        out_specs=[blk, blk],
        out_shape=[jax.ShapeDtypeStruct(zr.shape, BF16)] * 2,
        compiler_params=_params(("arbitrary", "arbitrary", "arbitrary")),
        name="dft_stage1",
    )(zr, zi, w1)


def _dft2_kernel(tr_ref, ti_ref, e_ref, y_ref):
    n2 = tr_ref.shape[0] // HALO
    ys = []
    for j in range(HALO):
        rows = slice(j * n2, (j + 1) * n2)
        t = jnp.concatenate([tr_ref[rows, :], ti_ref[rows, :]], axis=0)
        ys.append(_dot(e_ref[j], t))
    y_ref[...] = pltpu.einshape("jmc->mjc", jnp.stack(ys, axis=0))


def _dft_stage2(tr, ti, e):
    b, n, ch = tr.shape
    n_blk, _, k2, n2x2 = e.shape
    rows = HALO * n2x2 // 2
    t_spec = pl.BlockSpec((None, rows, ch), lambda bi, j: (bi, j, 0))
    return pl.pallas_call(
        _dft2_kernel,
        grid=(b, n_blk),
        in_specs=[t_spec, t_spec, pl.BlockSpec((None, HALO, k2, n2x2), lambda bi, j: (j, 0, 0, 0))],
        out_specs=pl.BlockSpec((None, k2, HALO, ch), lambda bi, j: (bi, 0, j, 0)),
        out_shape=jax.ShapeDtypeStruct((b, k2, n // k2, ch), F32),
        compiler_params=_params(("arbitrary", "arbitrary")),
        name="dft_stage2",
    )(tr, ti, e)


def _rope_tables(n):
    n_rows = n // GRID_W
    quarter = HEAD_DIM // 4
    freqs = ROPE_THETA ** (-jnp.arange(quarter, dtype=F32) / quarter)
    ar = jnp.arange(n_rows, dtype=F32)[:, None] * freqs
    ac = jnp.arange(GRID_W, dtype=F32)[:, None] * freqs

    def table(fr, fc):
        fr = jnp.broadcast_to(fr[:, None, :], (n_rows, GRID_W, 2 * quarter))
        fc = jnp.broadcast_to(fc[None, :, :], (n_rows, GRID_W, 2 * quarter))
        return jnp.concatenate([fr, fc], axis=-1).reshape(n, HEAD_DIM)

    cr, sr, cc, sc = jnp.cos(ar), jnp.sin(ar), jnp.cos(ac), jnp.sin(ac)
    cos_t = table(jnp.concatenate([cr, cr], axis=-1), jnp.concatenate([cc, cc], axis=-1))
    sin_t = table(jnp.concatenate([-sr, sr], axis=-1), jnp.concatenate([-sc, sc], axis=-1))
    return cos_t, sin_t


def _unit_angle(num, den):
    return (2.0 * math.pi / den) * (num % den).astype(F32)


def _dft_tables(n):
    n1 = 128
    n2 = n // n1
    ch = jnp.arange(FFT_GROUP, dtype=jnp.int32)
    a = _unit_angle(ch[:, None] * ch[None, :], FFT_GROUP)
    cs = jnp.concatenate([jnp.cos(a), -jnp.sin(a)], axis=1).astype(BF16)
    k1 = jnp.arange(n1, dtype=jnp.int32)
    a1 = _unit_angle(k1[:, None] * k1[None, :], n1)
    fr, fi = jnp.cos(a1), -jnp.sin(a1)
    w1 = jnp.concatenate([jnp.concatenate([fr, -fi], axis=1),
                          jnp.concatenate([fi, fr], axis=1)], axis=0).astype(BF16)
    m = jnp.arange(n2, dtype=jnp.int32)
    aa = _unit_angle(k1[:, None] * m[None, :], n // n1)
    ab = _unit_angle(k1[:, None] * m[None, :], n)
    ca, sa = jnp.cos(aa)[None, None, :, :], jnp.sin(aa)[None, None, :, :]
    n_blk = n1 // HALO
    cb = jnp.cos(ab).reshape(n_blk, HALO, 1, n2)
    sb = jnp.sin(ab).reshape(n_blk, HALO, 1, n2)
    norm = 1.0 / math.sqrt(n * FFT_GROUP)
    e2 = jnp.concatenate([(ca * cb - sa * sb) * norm, (sa * cb + ca * sb) * norm], axis=-1).astype(BF16)
    return cs, w1, e2


def kernel(x, c, ctx, c_ctx, norm_g, w_mod, b_mod, w_in_even, w_out_even, q_norm_g, k_norm_g, w_pool,
           pool_scale, w_in_odd, w_out_odd, conv_w, w_ffn_gate, w_ffn_up, w_ffn_down, final_g):
    b, n, d = x.shape
    tm = TILES.proj_rows

    cvec = jnp.zeros((8, d), F32).at[:b].set(c).at[b].set(c_ctx)
    mod = _modulation(cvec, w_mod, b_mod).reshape(w_mod.shape[0], 8, 6, 1, d)

    def mvec(layer, comp, rows=slice(0, b)):
        return mod[layer, rows, comp]

    zeros_vec = jnp.zeros((b, 1, d), F32)

    w0 = w_in_even[0]
    w_qku = w0.astype(BF16)
    w_vt = w_qku[:, ATTN_W + KV_W:ATTN_W + 2 * KV_W].T
    qg = q_norm_g[0].reshape(1, HEAD_DIM)
    kg = k_norm_g[0].reshape(1, HEAD_DIM)
    ng00 = norm_g[0, 0].reshape(1, d)
    cos_t, sin_t = _rope_tables(n)
    q, k, vt, u = _in_proj_even(x, mvec(0, 0), mvec(0, 1), ng00, w_qku, w_vt, qg, kg, cos_t, sin_t, tm)

    lc = ctx.shape[1]
    ctx_rows = slice(b, b + 1)
    ctx_shift = jnp.broadcast_to(mvec(0, 0, ctx_rows), (b, 1, d))
    ctx_scale = jnp.broadcast_to(mvec(0, 1, ctx_rows), (b, 1, d))
    ident_cos = jnp.ones((lc, HEAD_DIM), F32)
    ident_sin = jnp.zeros((lc, HEAD_DIM), F32)
    _, kc, vct, _ = _in_proj_even(ctx, ctx_shift, ctx_scale, ng00, w_qku, w_vt, qg, kg,
                                  ident_cos, ident_sin, lc)

    logit_bound = (math.sqrt(HEAD_DIM) * math.log2(math.e) * 1.02
                   * jnp.max(jnp.abs(q_norm_g[0])) * jnp.max(jnp.abs(k_norm_g[0])))
    attn = functools.partial(_attention, q, k, vt, kc, vct, tq=TILES.attn_queries, tk=TILES.attn_keys)
    o = lax.cond(logit_bound <= SAFE_LOGIT_BOUND,
                 lambda: attn(stabilize=False), lambda: attn(stabilize=True))

    x1, h1 = _out_proj_even(o, u, w_pool[0].astype(BF16), pool_scale[0].reshape(1, POOL_W),
                            w_out_even[0].astype(BF16), x, mvec(0, 2), norm_g[0, 1].reshape(1, d),
                            mvec(0, 3), mvec(0, 4), tm)
    x2, h2 = _ffn(h1, w_ffn_gate, w_ffn_up, 0, w_ffn_down[0].astype(BF16),
                  x1, mvec(0, 5), norm_g[1, 0].reshape(1, d), mvec(1, 0), mvec(1, 1),
                  emit_x=True)

    cs, w1, e2 = _dft_tables(n)
    zr, zi, cu, bg = _in_proj_odd(h2, w_in_odd[0].astype(BF16), cs, tm)
    n1 = 128
    tr, ti = _dft_stage1(zr.reshape(b, n1, n // n1, FFT_W), zi.reshape(b, n1, n // n1, FFT_W), w1,
                         TILES.dft_n2, TILES.dft_ch)
    y = _dft_stage2(tr.reshape(b, n, FFT_W), ti.reshape(b, n, FFT_W), e2).reshape(b, n, FFT_W)
    x3, h3 = _out_proj_odd(y, cu, bg, conv_w[0], w_out_odd[0].astype(BF16), x2, mvec(1, 2),
                           norm_g[1, 1].reshape(1, d), mvec(1, 3), mvec(1, 4), tm)
    (out,) = _ffn(h3, w_ffn_gate, w_ffn_up, 1, w_ffn_down[1].astype(BF16),
                  x3, mvec(1, 5), final_g.reshape(1, d), zeros_vec, zeros_vec,
                  emit_x=False)
    return out
```
